```python
import jax, jax.numpy as jnp
from jax import lax
import numpy as np

D_MODEL = 2048
BATCH = 16
SEQ = 2048
DEPTH = 1
DEC_BATCH = 8
DEC_SEQ = 16
PAST_LEN = 4096

CHUNK = 64
N_META = 16
D_MIX = D_MODEL
RWKV_DIM = D_MIX // 2
HEAD_DIM = 64
RWKV_HEADS = RWKV_DIM // HEAD_DIM
DECAY_LORA = 64
AAA_LORA = 64
GATE_LORA = 160
RWKV_COLS = 3 * RWKV_DIM + DECAY_LORA + AAA_LORA + GATE_LORA
CONV_CH = D_MIX - RWKV_DIM
CONV_W = 31
IN_COLS = RWKV_COLS + 2 * CONV_CH
D_FF = 5632
FFN_CONV_W = 3
RMS_EPS = 1e-6
LN_EPS = 1e-5
GN_EPS = 1e-5 * HEAD_DIM

kernel_name = "hymba_rwkv7_conformer_convffn_stream"


def rms_norm(x, g):
    xf = x.astype(jnp.float32)
    y = xf * lax.rsqrt(jnp.mean(xf * xf, -1, keepdims=True) + RMS_EPS)
    return (y * g.astype(jnp.float32)).astype(x.dtype)


def layer_norm(x, g, b):
    xf = x.astype(jnp.float32)
    mu = jnp.mean(xf, -1, keepdims=True)
    var = jnp.mean(jnp.square(xf - mu), -1, keepdims=True)
    y = (xf - mu) * lax.rsqrt(var + LN_EPS)
    return (y * g.astype(jnp.float32) + b.astype(jnp.float32)).astype(x.dtype)


def causal_dwconv(buf, x, w, b):
    full = jnp.concatenate([buf.astype(x.dtype), x], axis=1)
    y = lax.conv_general_dilated(full, w.astype(x.dtype)[:, None, :], window_strides=(1,),
                                 padding='VALID', dimension_numbers=('NWC', 'WIO', 'NWC'),
                                 feature_group_count=x.shape[-1])
    return y + b.astype(x.dtype), full[:, -(w.shape[0] - 1):, :]


def token_shift(prev, p, mu):
    p_prev = jnp.concatenate([prev.astype(p.dtype), p[:, :-1]], axis=1)
    return p + (p_prev - p) * mu.astype(p.dtype), p[:, -1:]


def wkv7_scan(S0, r, w, k, v, a_vec, b_vec):
    def step(S, inp):
        r_t, w_t, k_t, v_t, a_t, b_t = inp
        sa = jnp.einsum('bhvk,bhk->bhv', S, a_t)
        S = S * w_t[:, :, None, :] + sa[..., None] * b_t[:, :, None, :] + v_t[..., None] * k_t[:, :, None, :]
        return S, jnp.einsum('bhvk,bhk->bhv', S, r_t)
    xs = tuple(jnp.swapaxes(t, 0, 1) for t in (r, w, k, v, a_vec, b_vec))
    S, ys = lax.scan(step, S0, xs)
    return jnp.swapaxes(ys, 0, 1), S


def trunk_layer(x, shift_buf, wkv_state, conv_buf, ffn_buf,
                norm_mix, w_in, tshift_mu, w0, w_decay_up, a0, w_aaa_up, w_gate_up,
                k_k, k_a, r_k, lnx_g, lnx_b, conv_w, conv_b, conv_ln_g, conv_ln_b,
                w_out, norm_ffn, w_ffn_up, ffn_conv_w, ffn_conv_b, w_ffn_down):
    dt = x.dtype
    f32 = jnp.float32
    B, T, _ = x.shape
    h = rms_norm(x, norm_mix)
    proj = h @ w_in.astype(dt)
    p_rwkv, new_shift = token_shift(shift_buf, proj[..., :RWKV_COLS], tshift_mu)
    p_conv = proj[..., RWKV_COLS:]
    d3 = 3 * RWKV_DIM
    r, k, v, wl, al, gl = jnp.split(p_rwkv.astype(f32),
                                    [RWKV_DIM, 2 * RWKV_DIM, d3, d3 + DECAY_LORA, d3 + DECAY_LORA + AAA_LORA], axis=-1)
    w_log = -jax.nn.softplus(-(w0.astype(f32) + jnp.tanh(wl) @ w_decay_up.astype(f32))) - 0.5
    decay = jnp.exp(-jnp.exp(w_log))
    a = jax.nn.sigmoid(a0.astype(f32) + al @ w_aaa_up.astype(f32))
    g = jax.nn.sigmoid(gl) @ w_gate_up.astype(f32)
    heads = lambda t: t.reshape(B, T, RWKV_HEADS, HEAD_DIM)
    kk = heads(k * k_k.astype(f32))
    kk = kk / jnp.maximum(jnp.sqrt(jnp.sum(kk * kk, -1, keepdims=True)), 1e-12)
    k = k * (1.0 + (a - 1.0) * k_a.astype(f32))
    rh, kh, vh, ah = heads(r), heads(k), heads(v), heads(a)
    y, S = wkv7_scan(wkv_state.astype(f32), rh, heads(decay), kh, vh, -kk, kk * ah)
    mu = jnp.mean(y, -1, keepdims=True)
    var = jnp.mean(jnp.square(y - mu), -1, keepdims=True)
    y = ((y - mu) * lax.rsqrt(var + GN_EPS)).reshape(B, T, RWKV_DIM) * lnx_g.astype(f32) + lnx_b.astype(f32)
    bonus = jnp.sum(rh * kh * r_k.astype(f32), -1, keepdims=True) * vh
    y_rwkv = ((y + bonus.reshape(B, T, RWKV_DIM)) * g).astype(dt)
    c_val, c_gate = jnp.split(p_conv, 2, axis=-1)
    c = c_val * jax.nn.sigmoid(c_gate)
    c, new_conv = causal_dwconv(conv_buf, c, conv_w, conv_b)
    c = jax.nn.silu(layer_norm(c, conv_ln_g, conv_ln_b))
    x = x + jnp.concatenate([y_rwkv, c], -1) @ w_out.astype(dt)
    h = rms_norm(x, norm_ffn)
    u, z = jnp.split(h @ w_ffn_up.astype(dt), 2, axis=-1)
    u, new_ffn = causal_dwconv(ffn_buf, u, ffn_conv_w, ffn_conv_b)
    x = x + (jax.nn.gelu(u) * z) @ w_ffn_down.astype(dt)
    return x, (new_shift, S.astype(dt), new_conv, new_ffn)


def setup_inputs(seed: int = 0) -> dict:
    key = jax.random.key(seed)
    ks = jax.random.split(key, 32)
    nrm = lambda i, shape, s: s * jax.random.normal(ks[i], shape, jnp.float32)
    L = DEPTH
    return {
        "x_prompt": nrm(0, (BATCH, SEQ, D_MODEL), 1.0),
        "x_sample": nrm(1, (DEC_BATCH, DEC_SEQ, D_MODEL), 1.0),
        "state_shift": nrm(2, (L, DEC_BATCH, 1, RWKV_COLS), 1.0),
        "state_wkv": nrm(3, (L, DEC_BATCH, RWKV_HEADS, HEAD_DIM, HEAD_DIM), 0.3),
        "cache_conv": nrm(4, (L, DEC_BATCH, CONV_W - 1, CONV_CH), 0.5),
        "cache_ffn_conv": nrm(5, (L, DEC_BATCH, FFN_CONV_W - 1, D_FF), 1.0),
        "meta": nrm(6, (N_META, D_MODEL), 1.0),
        "norm_mix": 1.0 + nrm(7, (L, D_MODEL), 0.02),
        "w_in": nrm(8, (L, D_MODEL, IN_COLS), D_MODEL ** -0.5),
        "tshift_mu": jax.random.uniform(ks[9], (L, RWKV_COLS), jnp.float32),
        "w0": jax.random.uniform(ks[10], (L, RWKV_DIM), jnp.float32, -6.0, 0.0),
        "w_decay_up": nrm(11, (L, DECAY_LORA, RWKV_DIM), 0.1),
        "a0": nrm(12, (L, RWKV_DIM), 0.5),
        "w_aaa_up": nrm(13, (L, AAA_LORA, RWKV_DIM), 0.5 * AAA_LORA ** -0.5),
        "w_gate_up": nrm(14, (L, GATE_LORA, RWKV_DIM), GATE_LORA ** -0.5),
        "k_k": 0.85 + nrm(15, (L, RWKV_DIM), 0.05),
        "k_a": 1.0 + nrm(16, (L, RWKV_DIM), 0.05),
        "r_k": nrm(17, (L, RWKV_HEADS, HEAD_DIM), 0.1),
        "lnx_g": 1.0 + nrm(18, (L, RWKV_DIM), 0.02),
        "lnx_b": nrm(19, (L, RWKV_DIM), 0.02),
        "conv_w": nrm(20, (L, CONV_W, CONV_CH), CONV_W ** -0.5),
        "conv_b": nrm(21, (L, CONV_CH), 0.02),
        "conv_ln_g": 1.0 + nrm(22, (L, CONV_CH), 0.02),
        "conv_ln_b": nrm(23, (L, CONV_CH), 0.02),
        "w_out": nrm(24, (L, D_MIX, D_MODEL), D_MIX ** -0.5),
        "norm_ffn": 1.0 + nrm(25, (L, D_MODEL), 0.02),
        "w_ffn_up": nrm(26, (L, D_MODEL, 2 * D_FF), D_MODEL ** -0.5),
        "ffn_conv_w": nrm(27, (L, FFN_CONV_W, D_FF), FFN_CONV_W ** -0.5),
        "ffn_conv_b": nrm(28, (L, D_FF), 0.02),
        "w_ffn_down": nrm(29, (L, D_FF, D_MODEL), D_FF ** -0.5),
        "final_norm": 1.0 + nrm(30, (D_MODEL,), 0.02),
    }


def reference(x_prompt, x_sample, state_shift, state_wkv, cache_conv, cache_ffn_conv, meta,
              norm_mix, w_in, tshift_mu, w0, w_decay_up, a0, w_aaa_up, w_gate_up,
              k_k, k_a, r_k, lnx_g, lnx_b, conv_w, conv_b, conv_ln_g, conv_ln_b,
              w_out, norm_ffn, w_ffn_up, ffn_conv_w, ffn_conv_b, w_ffn_down, final_norm):
    dt = x_prompt.dtype
    bp = x_prompt.shape[0]
    xp = jnp.concatenate([jnp.broadcast_to(meta.astype(dt)[None], (bp, N_META, D_MODEL)), x_prompt], axis=1)
    xs = x_sample
    sp_shift, sp_wkv, sp_conv, sp_ffn = [], [], [], []
    ss_shift, ss_wkv, ss_conv, ss_ffn = [], [], [], []
    for l in range(DEPTH):
        weights = (norm_mix[l], w_in[l], tshift_mu[l], w0[l], w_decay_up[l], a0[l], w_aaa_up[l], w_gate_up[l],
                   k_k[l], k_a[l], r_k[l], lnx_g[l], lnx_b[l], conv_w[l], conv_b[l], conv_ln_g[l], conv_ln_b[l],
                   w_out[l], norm_ffn[l], w_ffn_up[l], ffn_conv_w[l], ffn_conv_b[l], w_ffn_down[l])
        xp, (a1, a2, a3, a4) = trunk_layer(
            xp, jnp.zeros((bp, 1, RWKV_COLS), dt), jnp.zeros((bp, RWKV_HEADS, HEAD_DIM, HEAD_DIM), dt),
            jnp.zeros((bp, CONV_W - 1, CONV_CH), dt), jnp.zeros((bp, FFN_CONV_W - 1, D_FF), dt), *weights)
        xs, (b1, b2, b3, b4) = trunk_layer(
            xs, state_shift[l], state_wkv[l], cache_conv[l], cache_ffn_conv[l], *weights)
        sp_shift.append(a1); sp_wkv.append(a2); sp_conv.append(a3); sp_ffn.append(a4)
        ss_shift.append(b1); ss_wkv.append(b2); ss_conv.append(b3); ss_ffn.append(b4)
    y_prompt = rms_norm(xp, final_norm)[:, N_META:]
    y_sample = rms_norm(xs, final_norm)
    return (y_prompt, y_sample,
            jnp.stack(sp_shift), jnp.stack(sp_wkv), jnp.stack(sp_conv), jnp.stack(sp_ffn),
            jnp.stack(ss_shift), jnp.stack(ss_wkv), jnp.stack(ss_conv), jnp.stack(ss_ffn))
```

```python
import functools
import math

import jax
import jax.numpy as jnp
from jax import lax
from jax.experimental import pallas as pl
from jax.experimental.pallas import tpu as pltpu

F32 = jnp.float32
BF16 = jnp.bfloat16

D_MODEL = 2048
RWKV_DIM = 1024
HEAD_DIM = 64
N_HEADS = RWKV_DIM // HEAD_DIM
DECAY_LORA = 64
AAA_LORA = 64
GATE_LORA = 160
RWKV_COLS = 3 * RWKV_DIM + DECAY_LORA + AAA_LORA + GATE_LORA
CONV_CH = 1024
CONV_W = 31
D_FF = 5632
FFN_CONV_W = 3
N_META = 16
RMS_EPS = 1e-6
LN_EPS = 1e-5
GN_EPS = 1e-5 * HEAD_DIM

MXU_W = 256
SUBLANES = 8
VMEM_LIMIT_BYTES = 56 * 1024 * 1024

GROUP_W = MXU_W
GROUP_HEADS = GROUP_W // HEAD_DIM
N_GROUPS = RWKV_DIM // GROUP_W
CHUNK = MXU_W // GROUP_HEADS
N_DOUBLINGS = int(math.log2(CHUNK)) - 1

COL_TILE = 512
LORA_W = COL_TILE
LA_OFF = 3 * RWKV_DIM
LA_W = DECAY_LORA + AAA_LORA
GL_OFF = LA_OFF + MXU_W
GL_W = MXU_W
P_COLS = 3 * RWKV_DIM + LORA_W
N_P_TILES = P_COLS // COL_TILE
GLU_W = COL_TILE // 2
N_GLU_TILES = CONV_CH // GLU_W
IN_COLS_PAD = P_COLS + 2 * CONV_CH
CONV_HALO = 32
CONV_RB = 32
FF_TILE = 512
N_FF_TILES = D_FF // FF_TILE


def _cparams(sem):
    return pltpu.CompilerParams(dimension_semantics=sem, vmem_limit_bytes=VMEM_LIMIT_BYTES)


def _dot(a, b):
    return jnp.dot(a, b, preferred_element_type=F32)


def _dot_nt(a, b):
    return lax.dot_general(a, b, (((1,), (1,)), ((), ())), preferred_element_type=F32)


def _dot_tn(a, b):
    return lax.dot_general(a, b, (((0,), (0,)), ((), ())), preferred_element_type=F32)


def _inproj_kernel(x_ref, g_ref, w_ref, p_ref, c_ref, h_ref):
    j = pl.program_id(1)

    @pl.when(j == 0)
    def _():
        x = x_ref[...]
        ms = jnp.mean(x * x, axis=-1, keepdims=True)
        h_ref[...] = (x * lax.rsqrt(ms + RMS_EPS) * g_ref[...]).astype(BF16)

    acc = _dot(h_ref[...], w_ref[...])

    @pl.when(j < N_P_TILES)
    def _():
        p_ref[...] = acc

    @pl.when(j >= N_P_TILES)
    def _():
        c_ref[...] = acc[:, :GLU_W] * jax.nn.sigmoid(acc[:, GLU_W:])


def _inproj(x2d, g, w_in_pad, tm):
    rows = x2d.shape[0]
    n_col = IN_COLS_PAD // COL_TILE
    return pl.pallas_call(
        _inproj_kernel,
        grid=(rows // tm, n_col),
        in_specs=[
            pl.BlockSpec((tm, D_MODEL), lambda i, j: (i, 0)),
            pl.BlockSpec((1, D_MODEL), lambda i, j: (0, 0)),
            pl.BlockSpec((D_MODEL, COL_TILE), lambda i, j: (0, j)),
        ],
        out_specs=[
            pl.BlockSpec((tm, COL_TILE), lambda i, j: (i, jnp.minimum(j, N_P_TILES - 1))),
            pl.BlockSpec((tm, GLU_W), lambda i, j: (i, jnp.maximum(j - N_P_TILES, 0))),
        ],
        out_shape=[
            jax.ShapeDtypeStruct((rows, P_COLS), F32),
            jax.ShapeDtypeStruct((rows, CONV_CH), F32),
        ],
        scratch_shapes=[pltpu.VMEM((tm, D_MODEL), BF16)],
        compiler_params=_cparams(("arbitrary", "arbitrary")),
        name="inproj",
    )(x2d, g, w_in_pad)


def _wkv_kernel(p_ref, shift0_ref, s0_ref, mu_ref, w0_ref, wdec_ref, a0_ref, waaa_ref, wgate_ref,
                kk_ref, ka_ref, rk_ref, lng_ref, lnb_ref, tri_ref, ones_ref, hmask_ref,
                y_ref, s_ref,
                prev_ref, rt_ref, at_ref, kt_ref, bt_ref, kh_ref, bh_ref, vv_ref, el_ref,
                yacc_ref, bonus_ref, gate_ref, *, tt, t_valid, t_total):
    t = pl.program_id(1)

    @pl.when(t == 0)
    def _():
        prev_ref[...] = shift0_ref[...]
        s_ref[...] = s0_ref[...]

    hmask_bf = hmask_ref[...]
    head_shift = HEAD_DIM.bit_length() - 1
    hmask = (jnp.right_shift(lax.broadcasted_iota(jnp.int32, (GROUP_W, GROUP_W), 0), head_shift)
             == jnp.right_shift(lax.broadcasted_iota(jnp.int32, (GROUP_W, GROUP_W), 1), head_shift))

    def headsum(x):
        parts = [_dot(x[:, g * GROUP_W:(g + 1) * GROUP_W].astype(BF16), hmask_bf) for g in range(N_GROUPS)]
        return jnp.concatenate(parts, axis=1)

    p = p_ref[...]
    row = lax.broadcasted_iota(jnp.int32, (tt, 1), 0)
    p_prev = jnp.where(row == 0, prev_ref[...], pltpu.roll(p, 1, 0))
    prev_ref[...] = p[tt - 1:tt, :]
    ps = p + (p_prev - p) * mu_ref[...]
    r = ps[:, 0:RWKV_DIM]
    k = ps[:, RWKV_DIM:2 * RWKV_DIM]
    v = ps[:, 2 * RWKV_DIM:3 * RWKV_DIM]
    la = ps[:, LA_OFF:LA_OFF + LA_W]
    gl = ps[:, GL_OFF:GL_OFF + GL_W]

    z = w0_ref[...] + _dot(jnp.tanh(la).astype(BF16), wdec_ref[...])
    lw = -math.exp(-0.5) * jax.nn.sigmoid(z)
    a_lr = jax.nn.sigmoid(a0_ref[...] + _dot(la.astype(BF16), waaa_ref[...]))
    gate_ref[...] = _dot(jax.nn.sigmoid(gl).astype(BF16), wgate_ref[...])

    kk = k * kk_ref[...]
    kkn = kk / jnp.maximum(jnp.sqrt(headsum(kk * kk)), 1e-12)
    k2 = k * (1.0 + (a_lr - 1.0) * ka_ref[...])
    bvec = kkn * a_lr
    avec = -kkn
    bonus_ref[...] = headsum(r * k2 * rk_ref[...]) * v

    if t_valid < t_total:
        valid = (t * tt + row) < t_valid
        lw = jnp.where(valid, lw, 0.0)
        k2 = jnp.where(valid, k2, 0.0)
        bvec = jnp.where(valid, bvec, 0.0)
        avec = jnp.where(valid, avec, 0.0)

    lw_hi = lw.astype(BF16)
    res = lw - lw_hi.astype(F32)
    lw_mid = res.astype(BF16)
    lw_lo = (res - lw_mid.astype(F32)).astype(BF16)
    tri = tri_ref[...]
    ones = ones_ref[...]
    cs = _dot(tri, lw_hi) + _dot(tri, lw_mid) + _dot(tri, lw_lo)
    cl = _dot(ones, lw_hi) + _dot(ones, lw_mid) + _dot(ones, lw_lo)
    ec = jnp.exp(cs)
    emc = jnp.exp(-cs)
    el = jnp.exp(cl)
    kt = k2 * emc
    bt = bvec * emc
    rt_ref[...] = (r * ec).astype(BF16)
    at_ref[...] = (avec * jnp.exp(cs - lw)).astype(BF16)
    kt_ref[...] = kt.astype(BF16)
    bt_ref[...] = bt.astype(BF16)
    kh_ref[...] = (kt * el).astype(BF16)
    bh_ref[...] = (bt * el).astype(BF16)
    vv_ref[...] = v.astype(BF16)
    el_ref[...] = el

    t_idx = lax.broadcasted_iota(jnp.int32, (CHUNK, GROUP_W), 0)
    s_idx = jnp.bitwise_and(lax.broadcasted_iota(jnp.int32, (CHUNK, GROUP_W), 1), CHUNK - 1)
    strict = t_idx > s_idx
    incl = t_idx >= s_idx
    eye = (t_idx == s_idx).astype(F32)

    def bdiag(zb):
        return jnp.concatenate([zb] * GROUP_HEADS, axis=0) * hmask_bf

    def chunk_body(c, carry):
        rows = pl.ds(pl.multiple_of(c * CHUNK, CHUNK), CHUNK)
        for g in range(N_GROUPS):
            cols = slice(g * GROUP_W, (g + 1) * GROUP_W)
            s_old = s_ref[g]
            rt = rt_ref[rows, cols]
            at = at_ref[rows, cols]
            vv = vv_ref[rows, cols]
            lhs2 = jnp.concatenate([at, rt], axis=0)
            gb = _dot_nt(lhs2, bdiag(bt_ref[rows, cols]))
            gk = _dot_nt(lhs2, bdiag(kt_ref[rows, cols]))
            sr = _dot_nt(lhs2, s_old.astype(BF16))
            a_ab = jnp.where(strict, gb[:CHUNK], 0.0)
            a_rb = jnp.where(incl, gb[CHUNK:], 0.0)
            a_ak = jnp.where(strict, gk[:CHUNK], 0.0)
            a_rk = jnp.where(incl, gk[CHUNK:], 0.0)
            bd_v = bdiag(vv)
            x = sr[:CHUNK] + _dot(a_ak.astype(BF16), bd_v)
            tinv = eye + a_ab
            pw = a_ab
            for _ in range(N_DOUBLINGS):
                pw_b = pw.astype(BF16)
                pw = _dot(pw_b, bdiag(pw_b))
                tinv = tinv + _dot(tinv.astype(BF16), bdiag(pw.astype(BF16)))
            u = _dot(tinv.astype(BF16), bdiag(x.astype(BF16)))
            u_b = u.astype(BF16)
            y = sr[CHUNK:] + _dot(a_rb.astype(BF16), bdiag(u_b)) + _dot(a_rk.astype(BF16), bd_v)
            uv = jnp.concatenate([u_b, vv], axis=0)
            bk = jnp.concatenate([bh_ref[rows, cols], kh_ref[rows, cols]], axis=0)
            upd = _dot_tn(uv, bk)
            w_l = el_ref[pl.ds(pl.multiple_of(c * CHUNK, CHUNK), SUBLANES), cols][0:1, :]
            s_ref[g] = s_old * w_l + jnp.where(hmask, upd, 0.0)
            yacc_ref[rows, cols] = y
        return carry

    lax.fori_loop(0, tt // CHUNK, chunk_body, 0)

    y = yacc_ref[...]
    mean = headsum(y) * (1.0 / HEAD_DIM)
    d = y - mean
    var = headsum(d * d) * (1.0 / HEAD_DIM)
    yn = d * lax.rsqrt(var + GN_EPS) * lng_ref[...] + lnb_ref[...]
    y_ref[...] = ((yn + bonus_ref[...]) * gate_ref[...]).astype(BF16)


def _wkv(p3, shift0, s0_bd, wts, tt, t_valid):
    b, t_total, _ = p3.shape
    idx = jnp.arange(tt)
    same_chunk = (idx[:, None] // CHUNK) == (idx[None, :] // CHUNK)
    tri = (same_chunk & (idx[None, :] <= idx[:, None])).astype(BF16)
    ones = same_chunk.astype(BF16)
    gidx = jnp.arange(GROUP_W) // HEAD_DIM
    hmask = (gidx[:, None] == gidx[None, :]).astype(BF16)

    def const(shape):
        nd = len(shape)
        return pl.BlockSpec(shape, lambda i, j, _nd=nd: (0,) * _nd)

    row = lambda w: const((1, w))
    kern = functools.partial(_wkv_kernel, tt=tt, t_valid=t_valid, t_total=t_total)
    act = lambda dt: pltpu.VMEM((tt, RWKV_DIM), dt)
    return pl.pallas_call(
        kern,
        grid=(b, t_total // tt),
        in_specs=[
            pl.BlockSpec((None, tt, P_COLS), lambda i, j: (i, j, 0)),
            pl.BlockSpec((None, 1, P_COLS), lambda i, j: (i, 0, 0)),
            pl.BlockSpec((None, N_GROUPS, GROUP_W, GROUP_W), lambda i, j: (i, 0, 0, 0)),
            row(P_COLS), row(RWKV_DIM), const((LA_W, RWKV_DIM)), row(RWKV_DIM), const((LA_W, RWKV_DIM)),
            const((GL_W, RWKV_DIM)), row(RWKV_DIM), row(RWKV_DIM), row(RWKV_DIM), row(RWKV_DIM), row(RWKV_DIM),
            const((tt, tt)), const((tt, tt)), const((GROUP_W, GROUP_W)),
        ],
        out_specs=[
            pl.BlockSpec((None, tt, RWKV_DIM), lambda i, j: (i, j, 0)),
            pl.BlockSpec((None, N_GROUPS, GROUP_W, GROUP_W), lambda i, j: (i, 0, 0, 0)),
        ],
        out_shape=[
            jax.ShapeDtypeStruct((b, t_total, RWKV_DIM), BF16),
            jax.ShapeDtypeStruct((b, N_GROUPS, GROUP_W, GROUP_W), F32),
        ],
        scratch_shapes=[
            pltpu.VMEM((1, P_COLS), F32),
            act(BF16), act(BF16), act(BF16), act(BF16), act(BF16), act(BF16), act(BF16),
            act(F32), act(F32), act(F32), act(F32),
        ],
        compiler_params=_cparams(("arbitrary", "arbitrary")),
        name="wkv",
    )(p3, shift0, s0_bd, wts["mu"], wts["w0"], wts["wdec"], wts["a0"], wts["waaa"], wts["wgate"],
      wts["k_k"], wts["k_a"], wts["r_k"], wts["lnx_g"], wts["lnx_b"], tri, ones, hmask)


def _mixout_kernel(y_ref, c_ref, cache_ref, cw_ref, cb_ref, lg_ref, lb_ref, wo_ref, x_ref, o_ref,
                   cbuf_ref, cact_ref, *, tm):
    t = pl.program_id(1)

    @pl.when(t == 0)
    def _():
        cbuf_ref[0:CONV_HALO, :] = cache_ref[...]

    @pl.when(t > 0)
    def _():
        cbuf_ref[0:CONV_HALO, :] = cbuf_ref[tm:tm + CONV_HALO, :]

    cbuf_ref[CONV_HALO:CONV_HALO + tm, :] = c_ref[...]
    lead = CONV_HALO - (CONV_W - 1)

    def block(rb, carry):
        base = pl.multiple_of(rb * CONV_RB, CONV_RB)
        win = cbuf_ref[pl.ds(base, CONV_RB + CONV_HALO), :]
        acc = cw_ref[0:1, :] * win[lead:lead + CONV_RB, :]
        for j in range(1, CONV_W):
            acc = acc + cw_ref[j:j + 1, :] * win[lead + j:lead + j + CONV_RB, :]
        acc = acc + cb_ref[...]
        mean = jnp.mean(acc, axis=-1, keepdims=True)
        d = acc - mean
        var = jnp.mean(d * d, axis=-1, keepdims=True)
        yn = d * lax.rsqrt(var + LN_EPS) * lg_ref[...] + lb_ref[...]
        cact_ref[pl.ds(base, CONV_RB), :] = (yn * jax.nn.sigmoid(yn)).astype(BF16)
        return carry

    lax.fori_loop(0, tm // CONV_RB, block, 0)

    mix = _dot(y_ref[...], wo_ref[0:RWKV_DIM, :]) + _dot(cact_ref[...], wo_ref[RWKV_DIM:RWKV_DIM + CONV_CH, :])
    o_ref[...] = x_ref[...] + mix


def _mixout(y3, c3, cache, wts, x3, tm):
    b, t_total, _ = x3.shape

    def const(shape):
        nd = len(shape)
        return pl.BlockSpec(shape, lambda i, j, _nd=nd: (0,) * _nd)

    return pl.pallas_call(
        functools.partial(_mixout_kernel, tm=tm),
        grid=(b, t_total // tm),
        in_specs=[
            pl.BlockSpec((None, tm, RWKV_DIM), lambda i, j: (i, j, 0)),
            pl.BlockSpec((None, tm, CONV_CH), lambda i, j: (i, j, 0)),
            pl.BlockSpec((None, CONV_HALO, CONV_CH), lambda i, j: (i, 0, 0)),
            const((CONV_HALO, CONV_CH)), const((1, CONV_CH)), const((1, CONV_CH)), const((1, CONV_CH)),
            const((RWKV_DIM + CONV_CH, D_MODEL)),
            pl.BlockSpec((None, tm, D_MODEL), lambda i, j: (i, j, 0)),
        ],
        out_specs=pl.BlockSpec((None, tm, D_MODEL), lambda i, j: (i, j, 0)),
        out_shape=jax.ShapeDtypeStruct((b, t_total, D_MODEL), F32),
        scratch_shapes=[
            pltpu.VMEM((tm + CONV_HALO, CONV_CH), F32),
            pltpu.VMEM((tm, CONV_CH), BF16),
        ],
        compiler_params=_cparams(("arbitrary", "arbitrary")),
        name="mixout",
    )(y3, c3, cache, wts["conv_w"], wts["conv_b"], wts["conv_ln_g"], wts["conv_ln_b"], wts["w_out"], x3)


def _gelu_tanh(x):
    return 0.5 * x * (1.0 + jnp.tanh(math.sqrt(2.0 / math.pi) * (x + 0.044715 * (x * x * x))))


def _ffn_kernel(x_ref, gn_ref, wu_ref, wz_ref, fw_ref, fb_ref, wd_ref, cache_ref, fn_ref,
                o_ref, tail_ref, h_ref, acc_ref, *, tm, tail_off):
    i = pl.program_id(1)
    j = pl.program_id(2)

    @pl.when((i == 0) & (j == 0))
    def _():
        tail_ref[...] = cache_ref[...]

    @pl.when(j == 0)
    def _():
        x = x_ref[...]
        ms = jnp.mean(x * x, axis=-1, keepdims=True)
        h_ref[...] = (x * lax.rsqrt(ms + RMS_EPS) * gn_ref[...]).astype(BF16)
        acc_ref[...] = jnp.zeros_like(acc_ref)

    h = h_ref[...]
    u = _dot(h, wu_ref[...])
    z = _dot(h, wz_ref[...])
    prev = tail_ref[j]
    row = lax.broadcasted_iota(jnp.int32, (tm, 1), 0)
    pm1 = prev[SUBLANES - 1:SUBLANES, :]
    pm2 = prev[SUBLANES - 2:SUBLANES - 1, :]
    u_m1 = jnp.where(row == 0, pm1, pltpu.roll(u, 1, 0))
    u_m2 = jnp.where(row == 0, pm2, jnp.where(row == 1, pm1, pltpu.roll(u, 2, 0)))
    uc = fw_ref[0:1, :] * u_m2 + fw_ref[1:2, :] * u_m1 + fw_ref[2:3, :] * u + fb_ref[...]

    tail_ref[j] = u[tail_off - SUBLANES:tail_off, :]

    act = (_gelu_tanh(uc) * z).astype(BF16)
    acc_ref[...] += _dot(act, wd_ref[...])

    @pl.when(j == pl.num_programs(2) - 1)
    def _():
        x2 = x_ref[...] + acc_ref[...]
        ms = jnp.mean(x2 * x2, axis=-1, keepdims=True)
        o_ref[...] = x2 * lax.rsqrt(ms + RMS_EPS) * fn_ref[...]


def _ffn(x3, cache, wts, tm, t_valid):
    b, t_total, _ = x3.shape
    n_i = t_total // tm
    assert t_valid == t_total or n_i == 1
    tail_off = t_valid if n_i == 1 else tm
    assert tail_off >= SUBLANES

    def const(shape):
        nd = len(shape)
        return pl.BlockSpec(shape, lambda s, i, j, _nd=nd: (0,) * _nd)

    kern = functools.partial(_ffn_kernel, tm=tm, tail_off=tail_off)
    return pl.pallas_call(
        kern,
        grid=(b, n_i, N_FF_TILES),
        in_specs=[
            pl.BlockSpec((None, tm, D_MODEL), lambda s, i, j: (s, i, 0)),
            const((1, D_MODEL)),
            pl.BlockSpec((D_MODEL, FF_TILE), lambda s, i, j: (0, j)),
            pl.BlockSpec((D_MODEL, FF_TILE), lambda s, i, j: (0, j)),
            pl.BlockSpec((SUBLANES, FF_TILE), lambda s, i, j: (0, j)),
            pl.BlockSpec((1, FF_TILE), lambda s, i, j: (0, j)),
            pl.BlockSpec((FF_TILE, D_MODEL), lambda s, i, j: (j, 0)),
            pl.BlockSpec((None, N_FF_TILES, SUBLANES, FF_TILE), lambda s, i, j: (s, 0, 0, 0)),
            const((1, D_MODEL)),
        ],
        out_specs=[
            pl.BlockSpec((None, tm, D_MODEL), lambda s, i, j: (s, i, 0)),
            pl.BlockSpec((None, N_FF_TILES, SUBLANES, FF_TILE), lambda s, i, j: (s, 0, 0, 0)),
        ],
        out_shape=[
            jax.ShapeDtypeStruct((b, t_total, D_MODEL), F32),
            jax.ShapeDtypeStruct((b, N_FF_TILES, SUBLANES, FF_TILE), F32),
        ],
        scratch_shapes=[
            pltpu.VMEM((tm, D_MODEL), BF16),
            pltpu.VMEM((tm, D_MODEL), F32),
        ],
        compiler_params=_cparams(("arbitrary", "arbitrary", "arbitrary")),
        name="ffn",
    )(x3, wts["norm_ffn"], wts["w_up_u"], wts["w_up_z"], wts["ffn_conv_w"], wts["ffn_conv_b"],
      wts["w_down"], cache, wts["final_norm"])


def _pad_cols_rwkv(a):
    d3 = 3 * RWKV_DIM
    lead = a.shape[:-1]
    zeros = lambda w: jnp.zeros(lead + (w,), a.dtype)
    return jnp.concatenate([
        a[..., :d3 + LA_W], zeros(GL_OFF - LA_OFF - LA_W),
        a[..., d3 + LA_W:], zeros(GL_W - GATE_LORA)], axis=-1)


def _unpad_cols_rwkv(a):
    d3 = 3 * RWKV_DIM
    return jnp.concatenate([a[..., :d3 + LA_W], a[..., GL_OFF:GL_OFF + GATE_LORA]], axis=-1)


def _prep_weights(norm_mix, w_in, tshift_mu, w0, w_decay_up, a0, w_aaa_up, w_gate_up, k_k, k_a, r_k,
                  lnx_g, lnx_b, conv_w, conv_b, conv_ln_g, conv_ln_b, w_out, norm_ffn, w_ffn_up,
                  ffn_conv_w, ffn_conv_b, w_ffn_down, final_norm):
    row = lambda a: a.reshape(1, -1).astype(F32)
    val = w_in[:, RWKV_COLS:RWKV_COLS + CONV_CH].reshape(D_MODEL, N_GLU_TILES, GLU_W)
    gate = w_in[:, RWKV_COLS + CONV_CH:].reshape(D_MODEL, N_GLU_TILES, GLU_W)
    glu = jnp.stack([val, gate], axis=2).reshape(D_MODEL, 2 * CONV_CH)
    w_in_pad = jnp.concatenate([_pad_cols_rwkv(w_in[:, :RWKV_COLS]), glu], axis=1).astype(BF16)
    zrows = lambda n: jnp.zeros((n, RWKV_DIM), F32)
    return {
        "norm_mix": row(norm_mix),
        "w_in": w_in_pad,
        "mu": row(_pad_cols_rwkv(tshift_mu)),
        "w0": row(w0),
        "wdec": jnp.concatenate([w_decay_up, zrows(AAA_LORA)], axis=0).astype(BF16),
        "a0": row(a0),
        "waaa": jnp.concatenate([zrows(DECAY_LORA), w_aaa_up], axis=0).astype(BF16),
        "wgate": jnp.concatenate([w_gate_up, zrows(GL_W - GATE_LORA)], axis=0).astype(BF16),
        "k_k": row(k_k), "k_a": row(k_a), "r_k": row(r_k), "lnx_g": row(lnx_g), "lnx_b": row(lnx_b),
        "conv_w": jnp.concatenate([conv_w, jnp.zeros((CONV_HALO - CONV_W, CONV_CH), F32)], axis=0).astype(F32),
        "conv_b": row(conv_b), "conv_ln_g": row(conv_ln_g), "conv_ln_b": row(conv_ln_b),
        "w_out": w_out.astype(BF16),
        "norm_ffn": row(norm_ffn),
        "w_up_u": w_ffn_up[:, :D_FF].astype(BF16),
        "w_up_z": w_ffn_up[:, D_FF:].astype(BF16),
        "ffn_conv_w": jnp.concatenate([ffn_conv_w, jnp.zeros((SUBLANES - FFN_CONV_W, D_FF), F32)], axis=0).astype(F32),
        "ffn_conv_b": row(ffn_conv_b),
        "w_down": w_ffn_down.astype(BF16),
        "final_norm": row(final_norm),
    }


def _state_to_blockdiag(s):
    b = s.shape[0]
    s5 = s.reshape(b, N_GROUPS, GROUP_HEADS, HEAD_DIM, HEAD_DIM).astype(F32)
    eye = jnp.eye(GROUP_HEADS, dtype=F32)
    bd = s5[:, :, :, :, None, :] * eye[None, None, :, None, :, None]
    return bd.reshape(b, N_GROUPS, GROUP_W, GROUP_W)


def _blockdiag_to_state(bd):
    b = bd.shape[0]
    s6 = bd.reshape(b, N_GROUPS, GROUP_HEADS, HEAD_DIM, GROUP_HEADS, HEAD_DIM)
    diag = jnp.stack([s6[:, :, h, :, h, :] for h in range(GROUP_HEADS)], axis=2)
    return diag.reshape(b, N_HEADS, HEAD_DIM, HEAD_DIM)


def _layer(x3, t_valid, shift, wkv_state, conv_cache, ffn_cache, wts, tiles):
    b, t_total, _ = x3.shape
    tm_in, tt, tm_mix, tm_ffn = tiles
    p2, c2 = _inproj(x3.reshape(b * t_total, D_MODEL), wts["norm_mix"], wts["w_in"], tm_in)
    p3 = p2.reshape(b, t_total, P_COLS)
    c3 = c2.reshape(b, t_total, CONV_CH)

    y3, s_bd = _wkv(p3, _pad_cols_rwkv(shift.astype(F32)), _state_to_blockdiag(wkv_state), wts, tt, t_valid)

    cache_pad = jnp.concatenate(
        [jnp.zeros((b, CONV_HALO - (CONV_W - 1), CONV_CH), F32), conv_cache.astype(F32)], axis=1)
    x1 = _mixout(y3, c3, cache_pad, wts, x3, tm_mix)

    fc = jnp.concatenate([jnp.zeros((b, SUBLANES - (FFN_CONV_W - 1), D_FF), F32), ffn_cache.astype(F32)], axis=1)
    fc = fc.reshape(b, SUBLANES, N_FF_TILES, FF_TILE).transpose(0, 2, 1, 3)
    y, tail = _ffn(x1, fc, wts, tm_ffn, t_valid)

    new_shift = _unpad_cols_rwkv(p3[:, t_valid - 1:t_valid, :])
    new_wkv = _blockdiag_to_state(s_bd)
    new_conv = jnp.concatenate([conv_cache.astype(F32), c3[:, :t_valid]], axis=1)[:, -(CONV_W - 1):]
    new_ffn = tail.transpose(0, 2, 1, 3).reshape(b, SUBLANES, D_FF)[:, SUBLANES - (FFN_CONV_W - 1):]
    return y, new_shift, new_wkv, new_conv, new_ffn


def _prompt_tiles(b, t):
    tiles = (min(1024, b * t), min(256, t), min(512, t), min(512, t))
    assert (b * t) % tiles[0] == 0 and all(t % x == 0 for x in tiles[1:]) and tiles[1] % CHUNK == 0
    return tiles


def kernel(x_prompt, x_sample, state_shift, state_wkv, cache_conv, cache_ffn_conv, meta, norm_mix, w_in,
           tshift_mu, w0, w_decay_up, a0, w_aaa_up, w_gate_up, k_k, k_a, r_k, lnx_g, lnx_b, conv_w, conv_b,
           conv_ln_g, conv_ln_b, w_out, norm_ffn, w_ffn_up, ffn_conv_w, ffn_conv_b, w_ffn_down, final_norm):
    depth = w_in.shape[0]
    bp, t_prompt, _ = x_prompt.shape
    bs, t_s, _ = x_sample.shape
    assert t_s == N_META, "the short streams (meta prefix, sample) share one padded pass"
    dt = x_prompt.dtype

    t_pad = CHUNK
    xs = jnp.concatenate([meta.astype(dt)[None], x_sample], axis=0)
    xs = jnp.pad(xs, ((0, 0), (0, t_pad - t_s), (0, 0)))
    xp = x_prompt
    nb = bs + 1

    outs_p, outs_s = [], []
    for l in range(depth):
        wts = _prep_weights(norm_mix[l], w_in[l], tshift_mu[l], w0[l], w_decay_up[l], a0[l], w_aaa_up[l],
                            w_gate_up[l], k_k[l], k_a[l], r_k[l], lnx_g[l], lnx_b[l], conv_w[l], conv_b[l],
                            conv_ln_g[l], conv_ln_b[l], w_out[l], norm_ffn[l], w_ffn_up[l], ffn_conv_w[l],
                            ffn_conv_b[l], w_ffn_down[l], final_norm)
        z = lambda *s: jnp.zeros(s, F32)
        st_s = (
            jnp.concatenate([z(1, 1, RWKV_COLS), state_shift[l]], axis=0),
            jnp.concatenate([z(1, N_HEADS, HEAD_DIM, HEAD_DIM), state_wkv[l]], axis=0),
            jnp.concatenate([z(1, CONV_W - 1, CONV_CH), cache_conv[l]], axis=0),
            jnp.concatenate([z(1, FFN_CONV_W - 1, D_FF), cache_ffn_conv[l]], axis=0),
        )
        ys, sh_s, wk_s, cv_s, ff_s = _layer(xs, t_s, *st_s, wts, (nb * t_pad, t_pad, t_pad, t_pad))
        bc = lambda a: jnp.broadcast_to(a[0:1], (bp,) + a.shape[1:])
        yp, sh_p, wk_p, cv_p, ff_p = _layer(xp, t_prompt, bc(sh_s), bc(wk_s), bc(cv_s), bc(ff_s), wts,
                                            _prompt_tiles(bp, t_prompt))
        outs_s.append((sh_s[1:], wk_s[1:], cv_s[1:], ff_s[1:]))
        outs_p.append((sh_p, wk_p, cv_p, ff_p))
        assert depth == 1
        xs, xp = ys, yp

    stack = lambda outs, i: jnp.stack([o[i] for o in outs]).astype(dt)
    y_prompt = xp
    y_sample = xs[1:, :t_s]
    return (y_prompt, y_sample,
            stack(outs_p, 0), stack(outs_p, 1), stack(outs_p, 2), stack(outs_p, 3),
            stack(outs_s, 0), stack(outs_s, 1), stack(outs_s, 2), stack(outs_s, 3))
```

```python
import functools
import math

import jax
import jax.numpy as jnp
from jax import lax
from jax.experimental import pallas as pl
from jax.experimental.pallas import tpu as pltpu

F32 = jnp.float32
BF16 = jnp.bfloat16

D_MODEL = 2048
RWKV_DIM = 1024
HEAD_DIM = 64
N_HEADS = RWKV_DIM // HEAD_DIM
DECAY_LORA = 64
AAA_LORA = 64
GATE_LORA = 160
RWKV_COLS = 3 * RWKV_DIM + DECAY_LORA + AAA_LORA + GATE_LORA
CONV_CH = 1024
CONV_W = 31
D_FF = 5632
FFN_CONV_W = 3
N_META = 16
RMS_EPS = 1e-6
LN_EPS = 1e-5
GN_EPS = 1e-5 * HEAD_DIM

MXU_W = 256
SUBLANES = 8
VMEM_LIMIT_BYTES = 56 * 1024 * 1024

GROUP_W = MXU_W
GROUP_HEADS = GROUP_W // HEAD_DIM
N_GROUPS = RWKV_DIM // GROUP_W
CHUNK = MXU_W // GROUP_HEADS
N_DOUBLINGS = int(math.log2(CHUNK)) - 1

COL_TILE = 512
LORA_W = COL_TILE
LA_OFF = 3 * RWKV_DIM
LA_W = DECAY_LORA + AAA_LORA
GL_OFF = LA_OFF + MXU_W
GL_W = MXU_W
P_COLS = 3 * RWKV_DIM + LORA_W
N_P_TILES = P_COLS // COL_TILE
GLU_W = COL_TILE // 2
N_GLU_TILES = CONV_CH // GLU_W
IN_COLS_PAD = P_COLS + 2 * CONV_CH
CONV_HALO = 32
CONV_RB = 32
FF_TILE = 512
N_FF_TILES = D_FF // FF_TILE


def _cparams(sem):
    return pltpu.CompilerParams(dimension_semantics=sem, vmem_limit_bytes=VMEM_LIMIT_BYTES)


def _dot(a, b):
    return jnp.dot(a, b, preferred_element_type=F32)


def _dot_nt(a, b):
    return lax.dot_general(a, b, (((1,), (1,)), ((), ())), preferred_element_type=F32)


def _dot_tn(a, b):
    return lax.dot_general(a, b, (((0,), (0,)), ((), ())), preferred_element_type=F32)


def _inproj_kernel(x_ref, g_ref, w_ref, p_ref, c_ref, h_ref):
    j = pl.program_id(1)

    @pl.when(j == 0)
    def _():
        x = x_ref[...]
        ms = jnp.mean(x * x, axis=-1, keepdims=True)
        h_ref[...] = (x * lax.rsqrt(ms + RMS_EPS) * g_ref[...]).astype(BF16)

    acc = _dot(h_ref[...], w_ref[...])

    @pl.when(j < N_P_TILES)
    def _():
        p_ref[...] = acc

    @pl.when(j >= N_P_TILES)
    def _():
        c_ref[...] = acc[:, :GLU_W] * jax.nn.sigmoid(acc[:, GLU_W:])


def _inproj(x2d, g, w_in_pad, tm):
    rows = x2d.shape[0]
    n_col = IN_COLS_PAD // COL_TILE
    return pl.pallas_call(
        _inproj_kernel,
        grid=(rows // tm, n_col),
        in_specs=[
            pl.BlockSpec((tm, D_MODEL), lambda i, j: (i, 0)),
            pl.BlockSpec((1, D_MODEL), lambda i, j: (0, 0)),
            pl.BlockSpec((D_MODEL, COL_TILE), lambda i, j: (0, j)),
        ],
        out_specs=[
            pl.BlockSpec((tm, COL_TILE), lambda i, j: (i, jnp.minimum(j, N_P_TILES - 1))),
            pl.BlockSpec((tm, GLU_W), lambda i, j: (i, jnp.maximum(j - N_P_TILES, 0))),
        ],
        out_shape=[
            jax.ShapeDtypeStruct((rows, P_COLS), F32),
            jax.ShapeDtypeStruct((rows, CONV_CH), F32),
        ],
        scratch_shapes=[pltpu.VMEM((tm, D_MODEL), BF16)],
        compiler_params=_cparams(("arbitrary", "arbitrary")),
        name="inproj",
    )(x2d, g, w_in_pad)


def _wkv_kernel(p_ref, shift0_ref, s0_ref, mu_ref, w0_ref, wdec_ref, a0_ref, waaa_ref, wgate_ref,
                kk_ref, ka_ref, rk_ref, lng_ref, lnb_ref, tri_ref, hmask_ref,
                y_ref, s_ref,
                prev_ref, rt_ref, at_ref, kt_ref, bt_ref, kh_ref, bh_ref, vv_ref, wa_ref, rp_ref, el_ref,
                yacc_ref, bonus_ref, gate_ref, u0_ref, *, tt, t_valid, t_total):
    t = pl.program_id(1)

    @pl.when(t == 0)
    def _():
        prev_ref[...] = shift0_ref[...]
        s_ref[...] = s0_ref[...]

    hmask_bf = hmask_ref[...]
    head_shift = HEAD_DIM.bit_length() - 1
    hmask = (jnp.right_shift(lax.broadcasted_iota(jnp.int32, (GROUP_W, GROUP_W), 0), head_shift)
             == jnp.right_shift(lax.broadcasted_iota(jnp.int32, (GROUP_W, GROUP_W), 1), head_shift))

    def headsum(x):
        parts = [_dot(x[:, g * GROUP_W:(g + 1) * GROUP_W].astype(BF16), hmask_bf) for g in range(N_GROUPS)]
        return jnp.concatenate(parts, axis=1)

    p = p_ref[...]
    row = lax.broadcasted_iota(jnp.int32, (tt, 1), 0)
    p_prev = jnp.where(row == 0, prev_ref[...], pltpu.roll(p, 1, 0))
    prev_ref[...] = p[tt - 1:tt, :]
    ps = p + (p_prev - p) * mu_ref[...]
    r = ps[:, 0:RWKV_DIM]
    k = ps[:, RWKV_DIM:2 * RWKV_DIM]
    v = ps[:, 2 * RWKV_DIM:3 * RWKV_DIM]
    la = ps[:, LA_OFF:LA_OFF + LA_W]
    gl = ps[:, GL_OFF:GL_OFF + GL_W]

    z = w0_ref[...] + _dot(jnp.tanh(la).astype(BF16), wdec_ref[...])
    lw = -math.exp(-0.5) * jax.nn.sigmoid(z)
    a_lr = jax.nn.sigmoid(a0_ref[...] + _dot(la.astype(BF16), waaa_ref[...]))
    gate_ref[...] = _dot(jax.nn.sigmoid(gl).astype(BF16), wgate_ref[...])

    kk = k * kk_ref[...]
    kkn = kk * lax.rsqrt(jnp.maximum(headsum(kk * kk), 1e-24))
    k2 = k * (1.0 + (a_lr - 1.0) * ka_ref[...])
    bvec = kkn * a_lr
    avec = -kkn
    bonus_ref[...] = headsum(r * k2 * rk_ref[...]) * v

    if t_valid < t_total:
        valid = (t * tt + row) < t_valid
        lw = jnp.where(valid, lw, 0.0)
        k2 = jnp.where(valid, k2, 0.0)
        bvec = jnp.where(valid, bvec, 0.0)
        avec = jnp.where(valid, avec, 0.0)

    lw_hi = lw.astype(BF16)
    lw_lo = (lw - lw_hi.astype(F32)).astype(BF16)
    tri = tri_ref[...]
    cs = _dot(tri, lw_hi) + _dot(tri, lw_lo)
    ec = jnp.exp(cs)
    emc = jnp.exp(-cs)
    kt = k2 * emc
    bt = bvec * emc
    rt_ref[...] = (r * ec).astype(BF16)
    at_ref[...] = (avec * jnp.exp(cs - lw)).astype(BF16)
    kt_ref[...] = kt.astype(BF16)
    bt_ref[...] = bt.astype(BF16)
    vv_ref[...] = v.astype(BF16)
    for c in range(tt // CHUNK):
        rows = slice(c * CHUNK, (c + 1) * CHUNK)
        el = ec[(c + 1) * CHUNK - 1:(c + 1) * CHUNK, :]
        el_ref[c:c + 1, :] = el
        kh_ref[rows, :] = (kt[rows, :] * el).astype(BF16)
        bh_ref[rows, :] = (bt[rows, :] * el).astype(BF16)

    t_idx = lax.broadcasted_iota(jnp.int32, (CHUNK, GROUP_W), 0)
    s_idx = jnp.bitwise_and(lax.broadcasted_iota(jnp.int32, (CHUNK, GROUP_W), 1), CHUNK - 1)
    strict = t_idx > s_idx
    incl = t_idx >= s_idx
    eye = (t_idx == s_idx).astype(F32)

    def bdiag(zb):
        return jnp.concatenate([zb] * GROUP_HEADS, axis=0) * hmask_bf

    groups = range(N_GROUPS)
    cols = [slice(g * GROUP_W, (g + 1) * GROUP_W) for g in groups]
    rowcat = lambda a, b: jnp.concatenate([a, b], axis=0)

    for c in range(tt // CHUNK):
        rows = slice(c * CHUNK, (c + 1) * CHUNK)
        rt = [rt_ref[rows, cols[g]] for g in groups]
        at = [at_ref[rows, cols[g]] for g in groups]
        bd_v = [bdiag(vv_ref[rows, cols[g]]) for g in groups]
        lhs2 = [rowcat(at[g], rt[g]) for g in groups]
        gb = [_dot_nt(lhs2[g], bdiag(bt_ref[rows, cols[g]])) for g in groups]
        gk = [_dot_nt(lhs2[g], bdiag(kt_ref[rows, cols[g]])) for g in groups]
        a_ab = [jnp.where(strict, gb[g][:CHUNK], 0.0) for g in groups]
        a_rb = [jnp.where(incl, gb[g][CHUNK:], 0.0).astype(BF16) for g in groups]
        a_kk = [rowcat(jnp.where(strict, gk[g][:CHUNK], 0.0), jnp.where(incl, gk[g][CHUNK:], 0.0)).astype(BF16)
                for g in groups]
        pw_b = [a_ab[g].astype(BF16) for g in groups]
        tinv = [eye + a_ab[g] for g in groups]
        pw_b = [_dot(pw_b[g], bdiag(pw_b[g])).astype(BF16) for g in groups]
        for _ in range(N_DOUBLINGS - 1):
            res = [_dot(rowcat(pw_b[g], tinv[g].astype(BF16)), bdiag(pw_b[g])) for g in groups]
            pw_b = [res[g][:CHUNK].astype(BF16) for g in groups]
            tinv = [tinv[g] + res[g][CHUNK:] for g in groups]
        t_b = [(tinv[g] + _dot(tinv[g].astype(BF16), bdiag(pw_b[g]))).astype(BF16) for g in groups]
        av2 = [_dot(a_kk[g], bd_v[g]) for g in groups]
        w_a = [_dot(t_b[g], bdiag(at[g])).astype(BF16) for g in groups]
        u0 = [_dot(t_b[g], bdiag(av2[g][:CHUNK].astype(BF16))) for g in groups]
        for g in groups:
            wa_ref[rows, cols[g]] = w_a[g]
            u0_ref[rows, cols[g]] = u0[g]
            rp_ref[rows, cols[g]] = (rt[g].astype(F32) + _dot(a_rb[g], bdiag(w_a[g]))).astype(BF16)
            yacc_ref[rows, cols[g]] = _dot(a_rb[g], bdiag(u0[g].astype(BF16))) + av2[g][CHUNK:]

    for c in range(tt // CHUNK):
        rows = slice(c * CHUNK, (c + 1) * CHUNK)
        s_old = [s_ref[g] for g in groups]
        res = [_dot_nt(rowcat(wa_ref[rows, cols[g]], rp_ref[rows, cols[g]]), s_old[g].astype(BF16)) for g in groups]
        u_b = [(res[g][:CHUNK] + u0_ref[rows, cols[g]]).astype(BF16) for g in groups]
        upd = [_dot_tn(rowcat(u_b[g], vv_ref[rows, cols[g]]), rowcat(bh_ref[rows, cols[g]], kh_ref[rows, cols[g]]))
               for g in groups]
        for g in groups:
            yacc_ref[rows, cols[g]] = res[g][CHUNK:] + yacc_ref[rows, cols[g]]
            w_l = el_ref[c:c + 1, cols[g]]
            s_ref[g] = s_old[g] * w_l + jnp.where(hmask, upd[g], 0.0)

    y = yacc_ref[...]
    mean = headsum(y) * (1.0 / HEAD_DIM)
    d = y - mean
    var = headsum(d * d) * (1.0 / HEAD_DIM)
    yn = d * lax.rsqrt(var + GN_EPS) * lng_ref[...] + lnb_ref[...]
    y_ref[...] = ((yn + bonus_ref[...]) * gate_ref[...]).astype(BF16)


def _wkv(p3, shift0, s0_bd, wts, tt, t_valid):
    b, t_total, _ = p3.shape
    idx = jnp.arange(tt)
    same_chunk = (idx[:, None] // CHUNK) == (idx[None, :] // CHUNK)
    tri = (same_chunk & (idx[None, :] <= idx[:, None])).astype(BF16)
    gidx = jnp.arange(GROUP_W) // HEAD_DIM
    hmask = (gidx[:, None] == gidx[None, :]).astype(BF16)

    def const(shape):
        nd = len(shape)
        return pl.BlockSpec(shape, lambda i, j, _nd=nd: (0,) * _nd)

    row = lambda w: const((1, w))
    kern = functools.partial(_wkv_kernel, tt=tt, t_valid=t_valid, t_total=t_total)
    act = lambda dt: pltpu.VMEM((tt, RWKV_DIM), dt)
    return pl.pallas_call(
        kern,
        grid=(b, t_total // tt),
        in_specs=[
            pl.BlockSpec((None, tt, P_COLS), lambda i, j: (i, j, 0)),
            pl.BlockSpec((None, 1, P_COLS), lambda i, j: (i, 0, 0)),
            pl.BlockSpec((None, N_GROUPS, GROUP_W, GROUP_W), lambda i, j: (i, 0, 0, 0)),
            row(P_COLS), row(RWKV_DIM), const((LA_W, RWKV_DIM)), row(RWKV_DIM), const((LA_W, RWKV_DIM)),
            const((GL_W, RWKV_DIM)), row(RWKV_DIM), row(RWKV_DIM), row(RWKV_DIM), row(RWKV_DIM), row(RWKV_DIM),
            const((tt, tt)), const((GROUP_W, GROUP_W)),
        ],
        out_specs=[
            pl.BlockSpec((None, tt, RWKV_DIM), lambda i, j: (i, j, 0)),
            pl.BlockSpec((None, N_GROUPS, GROUP_W, GROUP_W), lambda i, j: (i, 0, 0, 0)),
        ],
        out_shape=[
            jax.ShapeDtypeStruct((b, t_total, RWKV_DIM), BF16),
            jax.ShapeDtypeStruct((b, N_GROUPS, GROUP_W, GROUP_W), F32),
        ],
        scratch_shapes=[
            pltpu.VMEM((1, P_COLS), F32),
            act(BF16), act(BF16), act(BF16), act(BF16), act(BF16), act(BF16), act(BF16), act(BF16), act(BF16),
            pltpu.VMEM((max(tt // CHUNK, SUBLANES), RWKV_DIM), F32),
            act(F32), act(F32), act(F32), act(F32),
        ],
        compiler_params=_cparams(("arbitrary", "arbitrary")),
        name="wkv",
    )(p3, shift0, s0_bd, wts["mu"], wts["w0"], wts["wdec"], wts["a0"], wts["waaa"], wts["wgate"],
      wts["k_k"], wts["k_a"], wts["r_k"], wts["lnx_g"], wts["lnx_b"], tri, hmask)


def _mixout_kernel(y_ref, c_ref, cache_ref, cw_ref, cb_ref, lg_ref, lb_ref, wo_ref, x_ref, o_ref,
                   cbuf_ref, cact_ref, *, tm):
    t = pl.program_id(1)

    @pl.when(t == 0)
    def _():
        cbuf_ref[0:CONV_HALO, :] = cache_ref[...]

    @pl.when(t > 0)
    def _():
        cbuf_ref[0:CONV_HALO, :] = cbuf_ref[tm:tm + CONV_HALO, :]

    cbuf_ref[CONV_HALO:CONV_HALO + tm, :] = c_ref[...]
    lead = CONV_HALO - (CONV_W - 1)

    def block(rb, carry):
        base = pl.multiple_of(rb * CONV_RB, CONV_RB)
        n_win = CONV_RB + CONV_HALO
        win = cbuf_ref[pl.ds(base, n_win), :]
        acc = None
        for sub in range(SUBLANES):
            taps = [j for j in range(CONV_W) if (lead + j) % SUBLANES == sub]
            if not taps:
                continue
            shifted = win if sub == 0 else pltpu.roll(win, n_win - sub, 0)
            for j in taps:
                off = lead + j - sub
                term = cw_ref[j:j + 1, :] * shifted[off:off + CONV_RB, :]
                acc = term if acc is None else acc + term
        acc = acc + cb_ref[...]
        mean = jnp.mean(acc, axis=-1, keepdims=True)
        d = acc - mean
        var = jnp.mean(d * d, axis=-1, keepdims=True)
        yn = d * lax.rsqrt(var + LN_EPS) * lg_ref[...] + lb_ref[...]
        cact_ref[pl.ds(base, CONV_RB), :] = (yn * jax.nn.sigmoid(yn)).astype(BF16)
        return carry

    lax.fori_loop(0, tm // CONV_RB, block, 0)

    mix = _dot(y_ref[...], wo_ref[0:RWKV_DIM, :]) + _dot(cact_ref[...], wo_ref[RWKV_DIM:RWKV_DIM + CONV_CH, :])
    o_ref[...] = x_ref[...] + mix


def _mixout(y3, c3, cache, wts, x3, tm):
    b, t_total, _ = x3.shape

    def const(shape):
        nd = len(shape)
        return pl.BlockSpec(shape, lambda i, j, _nd=nd: (0,) * _nd)

    return pl.pallas_call(
        functools.partial(_mixout_kernel, tm=tm),
        grid=(b, t_total // tm),
        in_specs=[
            pl.BlockSpec((None, tm, RWKV_DIM), lambda i, j: (i, j, 0)),
            pl.BlockSpec((None, tm, CONV_CH), lambda i, j: (i, j, 0)),
            pl.BlockSpec((None, CONV_HALO, CONV_CH), lambda i, j: (i, 0, 0)),
            const((CONV_HALO, CONV_CH)), const((1, CONV_CH)), const((1, CONV_CH)), const((1, CONV_CH)),
            const((RWKV_DIM + CONV_CH, D_MODEL)),
            pl.BlockSpec((None, tm, D_MODEL), lambda i, j: (i, j, 0)),
        ],
        out_specs=pl.BlockSpec((None, tm, D_MODEL), lambda i, j: (i, j, 0)),
        out_shape=jax.ShapeDtypeStruct((b, t_total, D_MODEL), F32),
        scratch_shapes=[
            pltpu.VMEM((tm + CONV_HALO, CONV_CH), F32),
            pltpu.VMEM((tm, CONV_CH), BF16),
        ],
        compiler_params=_cparams(("arbitrary", "arbitrary")),
        name="mixout",
    )(y3, c3, cache, wts["conv_w"], wts["conv_b"], wts["conv_ln_g"], wts["conv_ln_b"], wts["w_out"], x3)


def _gelu_tanh(x):
    return 0.5 * x * (1.0 + jnp.tanh(math.sqrt(2.0 / math.pi) * (x + 0.044715 * (x * x * x))))


def _ffn_kernel(x_ref, gn_ref, wu_ref, wz_ref, fw_ref, fb_ref, wd_ref, cache_ref, fn_ref,
                o_ref, tail_ref, h_ref, acc_ref, *, tm, n_seq, seq_rows, tail_off):
    i = pl.program_id(1)
    j = pl.program_id(2)

    @pl.when((i == 0) & (j == 0))
    def _():
        tail_ref[...] = cache_ref[...]

    @pl.when(j == 0)
    def _():
        x = x_ref[...]
        ms = jnp.mean(x * x, axis=-1, keepdims=True)
        h_ref[...] = (x * lax.rsqrt(ms + RMS_EPS) * gn_ref[...]).astype(BF16)
        acc_ref[...] = jnp.zeros_like(acc_ref)

    h = h_ref[...]
    u = _dot(h, wu_ref[...])
    z = _dot(h, wz_ref[...])
    prev = tail_ref[j]
    row = lax.broadcasted_iota(jnp.int32, (tm, 1), 0)
    u_m1 = pltpu.roll(u, 1, 0)
    u_m2 = pltpu.roll(u, 2, 0)
    for s in range(n_seq):
        pm1 = prev[(s + 1) * SUBLANES - 1:(s + 1) * SUBLANES, :]
        pm2 = prev[(s + 1) * SUBLANES - 2:(s + 1) * SUBLANES - 1, :]
        u_m1 = jnp.where(row == s * seq_rows, pm1, u_m1)
        u_m2 = jnp.where(row == s * seq_rows, pm2, jnp.where(row == s * seq_rows + 1, pm1, u_m2))
    uc = fw_ref[0:1, :] * u_m2 + fw_ref[1:2, :] * u_m1 + fw_ref[2:3, :] * u + fb_ref[...]

    for s in range(n_seq):
        end = s * seq_rows + tail_off
        tail_ref[j, s * SUBLANES:(s + 1) * SUBLANES, :] = u[end - SUBLANES:end, :]

    act = (_gelu_tanh(uc) * z).astype(BF16)
    acc_ref[...] += _dot(act, wd_ref[...])

    @pl.when(j == pl.num_programs(2) - 1)
    def _():
        x2 = x_ref[...] + acc_ref[...]
        ms = jnp.mean(x2 * x2, axis=-1, keepdims=True)
        o_ref[...] = x2 * lax.rsqrt(ms + RMS_EPS) * fn_ref[...]


def _ffn(x3, cache, wts, tm, t_valid):
    b0, t0, _ = x3.shape
    n_seq = max(tm // t0, 1)
    assert b0 % n_seq == 0 and (tm % t0 == 0 or t0 % tm == 0)
    b, t_total = b0 // n_seq, t0 * n_seq
    x3 = x3.reshape(b, t_total, D_MODEL)
    n_i = t_total // tm
    assert t_valid == t0 or n_i == 1
    tail_off = t_valid if n_i == 1 else tm
    assert tail_off >= SUBLANES
    tail_rows = n_seq * SUBLANES
    cache = jnp.concatenate([jnp.zeros((b0, SUBLANES - (FFN_CONV_W - 1), D_FF), F32), cache.astype(F32)], axis=1)
    cache = cache.reshape(b, tail_rows, N_FF_TILES, FF_TILE).transpose(0, 2, 1, 3)

    def const(shape):
        nd = len(shape)
        return pl.BlockSpec(shape, lambda s, i, j, _nd=nd: (0,) * _nd)

    kern = functools.partial(_ffn_kernel, tm=tm, n_seq=n_seq, seq_rows=t0, tail_off=tail_off)
    y, tail = pl.pallas_call(
        kern,
        grid=(b, n_i, N_FF_TILES),
        in_specs=[
            pl.BlockSpec((None, tm, D_MODEL), lambda s, i, j: (s, i, 0)),
            const((1, D_MODEL)),
            pl.BlockSpec((D_MODEL, FF_TILE), lambda s, i, j: (0, j)),
            pl.BlockSpec((D_MODEL, FF_TILE), lambda s, i, j: (0, j)),
            pl.BlockSpec((SUBLANES, FF_TILE), lambda s, i, j: (0, j)),
            pl.BlockSpec((1, FF_TILE), lambda s, i, j: (0, j)),
            pl.BlockSpec((FF_TILE, D_MODEL), lambda s, i, j: (j, 0)),
            pl.BlockSpec((None, N_FF_TILES, tail_rows, FF_TILE), lambda s, i, j: (s, 0, 0, 0)),
            const((1, D_MODEL)),
        ],
        out_specs=[
            pl.BlockSpec((None, tm, D_MODEL), lambda s, i, j: (s, i, 0)),
            pl.BlockSpec((None, N_FF_TILES, tail_rows, FF_TILE), lambda s, i, j: (s, 0, 0, 0)),
        ],
        out_shape=[
            jax.ShapeDtypeStruct((b, t_total, D_MODEL), F32),
            jax.ShapeDtypeStruct((b, N_FF_TILES, tail_rows, FF_TILE), F32),
        ],
        scratch_shapes=[
            pltpu.VMEM((tm, D_MODEL), BF16),
            pltpu.VMEM((tm, D_MODEL), F32),
        ],
        compiler_params=_cparams(("arbitrary", "arbitrary", "arbitrary")),
        name="ffn",
    )(x3, wts["norm_ffn"], wts["w_up_u"], wts["w_up_z"], wts["ffn_conv_w"], wts["ffn_conv_b"],
      wts["w_down"], cache, wts["final_norm"])
    new_cache = tail.transpose(0, 2, 1, 3).reshape(b0, SUBLANES, D_FF)[:, SUBLANES - (FFN_CONV_W - 1):]
    return y.reshape(b0, t0, D_MODEL), new_cache


def _pad_cols_rwkv(a):
    d3 = 3 * RWKV_DIM
    lead = a.shape[:-1]
    zeros = lambda w: jnp.zeros(lead + (w,), a.dtype)
    return jnp.concatenate([
        a[..., :d3 + LA_W], zeros(GL_OFF - LA_OFF - LA_W),
        a[..., d3 + LA_W:], zeros(GL_W - GATE_LORA)], axis=-1)


def _unpad_cols_rwkv(a):
    d3 = 3 * RWKV_DIM
    return jnp.concatenate([a[..., :d3 + LA_W], a[..., GL_OFF:GL_OFF + GATE_LORA]], axis=-1)


def _prep_weights(norm_mix, w_in, tshift_mu, w0, w_decay_up, a0, w_aaa_up, w_gate_up, k_k, k_a, r_k,
                  lnx_g, lnx_b, conv_w, conv_b, conv_ln_g, conv_ln_b, w_out, norm_ffn, w_ffn_up,
                  ffn_conv_w, ffn_conv_b, w_ffn_down, final_norm):
    row = lambda a: a.reshape(1, -1).astype(F32)
    val = w_in[:, RWKV_COLS:RWKV_COLS + CONV_CH].reshape(D_MODEL, N_GLU_TILES, GLU_W)
    gate = w_in[:, RWKV_COLS + CONV_CH:].reshape(D_MODEL, N_GLU_TILES, GLU_W)
    glu = jnp.stack([val, gate], axis=2).reshape(D_MODEL, 2 * CONV_CH)
    w_in_pad = jnp.concatenate([_pad_cols_rwkv(w_in[:, :RWKV_COLS]), glu], axis=1).astype(BF16)
    zrows = lambda n: jnp.zeros((n, RWKV_DIM), F32)
    return {
        "norm_mix": row(norm_mix),
        "w_in": w_in_pad,
        "mu": row(_pad_cols_rwkv(tshift_mu)),
        "w0": row(w0),
        "wdec": jnp.concatenate([w_decay_up, zrows(AAA_LORA)], axis=0).astype(BF16),
        "a0": row(a0),
        "waaa": jnp.concatenate([zrows(DECAY_LORA), w_aaa_up], axis=0).astype(BF16),
        "wgate": jnp.concatenate([w_gate_up, zrows(GL_W - GATE_LORA)], axis=0).astype(BF16),
        "k_k": row(k_k), "k_a": row(k_a), "r_k": row(r_k), "lnx_g": row(lnx_g), "lnx_b": row(lnx_b),
        "conv_w": jnp.concatenate([conv_w, jnp.zeros((CONV_HALO - CONV_W, CONV_CH), F32)], axis=0).astype(F32),
        "conv_b": row(conv_b), "conv_ln_g": row(conv_ln_g), "conv_ln_b": row(conv_ln_b),
        "w_out": w_out.astype(BF16),
        "norm_ffn": row(norm_ffn),
        "w_up_u": w_ffn_up[:, :D_FF].astype(BF16),
        "w_up_z": w_ffn_up[:, D_FF:].astype(BF16),
        "ffn_conv_w": jnp.concatenate([ffn_conv_w, jnp.zeros((SUBLANES - FFN_CONV_W, D_FF), F32)], axis=0).astype(F32),
        "ffn_conv_b": row(ffn_conv_b),
        "w_down": w_ffn_down.astype(BF16),
        "final_norm": row(final_norm),
    }


def _state_to_blockdiag(s):
    b = s.shape[0]
    s5 = s.reshape(b, N_GROUPS, GROUP_HEADS, HEAD_DIM, HEAD_DIM).astype(F32)
    eye = jnp.eye(GROUP_HEADS, dtype=F32)
    bd = s5[:, :, :, :, None, :] * eye[None, None, :, None, :, None]
    return bd.reshape(b, N_GROUPS, GROUP_W, GROUP_W)


def _blockdiag_to_state(bd):
    b = bd.shape[0]
    s6 = bd.reshape(b, N_GROUPS, GROUP_HEADS, HEAD_DIM, GROUP_HEADS, HEAD_DIM)
    diag = jnp.stack([s6[:, :, h, :, h, :] for h in range(GROUP_HEADS)], axis=2)
    return diag.reshape(b, N_HEADS, HEAD_DIM, HEAD_DIM)


def _layer(x3, t_valid, shift, wkv_state, conv_cache, ffn_cache, wts, tiles):
    b, t_total, _ = x3.shape
    tm_in, tt, tm_mix, tm_ffn = tiles
    p2, c2 = _inproj(x3.reshape(b * t_total, D_MODEL), wts["norm_mix"], wts["w_in"], tm_in)
    p3 = p2.reshape(b, t_total, P_COLS)
    c3 = c2.reshape(b, t_total, CONV_CH)

    y3, s_bd = _wkv(p3, _pad_cols_rwkv(shift.astype(F32)), _state_to_blockdiag(wkv_state), wts, tt, t_valid)

    cache_pad = jnp.concatenate(
        [jnp.zeros((b, CONV_HALO - (CONV_W - 1), CONV_CH), F32), conv_cache.astype(F32)], axis=1)
    x1 = _mixout(y3, c3, cache_pad, wts, x3, tm_mix)

    y, new_ffn = _ffn(x1, ffn_cache, wts, tm_ffn, t_valid)

    new_shift = _unpad_cols_rwkv(p3[:, t_valid - 1:t_valid, :])
    new_wkv = _blockdiag_to_state(s_bd)
    new_conv = jnp.concatenate([conv_cache.astype(F32), c3[:, :t_valid]], axis=1)[:, -(CONV_W - 1):]
    return y, new_shift, new_wkv, new_conv, new_ffn


def _prompt_tiles(b, t):
    tiles = (min(1024, b * t), min(256, t), min(512, t), min(512, t))
    assert (b * t) % tiles[0] == 0 and all(t % x == 0 for x in tiles[1:]) and tiles[1] % CHUNK == 0
    return tiles


def kernel(x_prompt, x_sample, state_shift, state_wkv, cache_conv, cache_ffn_conv, meta, norm_mix, w_in,
           tshift_mu, w0, w_decay_up, a0, w_aaa_up, w_gate_up, k_k, k_a, r_k, lnx_g, lnx_b, conv_w, conv_b,
           conv_ln_g, conv_ln_b, w_out, norm_ffn, w_ffn_up, ffn_conv_w, ffn_conv_b, w_ffn_down, final_norm):
    depth = w_in.shape[0]
    bp, t_prompt, _ = x_prompt.shape
    bs, t_s, _ = x_sample.shape
    assert t_s == N_META, "the short streams (meta prefix, sample) share one padded pass"
    dt = x_prompt.dtype

    t_pad = CHUNK
    xs = jnp.concatenate([meta.astype(dt)[None], x_sample], axis=0)
    xs = jnp.pad(xs, ((0, 0), (0, t_pad - t_s), (0, 0)))
    xp = x_prompt
    nb = bs + 1

    outs_p, outs_s = [], []
    for l in range(depth):
        wts = _prep_weights(norm_mix[l], w_in[l], tshift_mu[l], w0[l], w_decay_up[l], a0[l], w_aaa_up[l],
                            w_gate_up[l], k_k[l], k_a[l], r_k[l], lnx_g[l], lnx_b[l], conv_w[l], conv_b[l],
                            conv_ln_g[l], conv_ln_b[l], w_out[l], norm_ffn[l], w_ffn_up[l], ffn_conv_w[l],
                            ffn_conv_b[l], w_ffn_down[l], final_norm)
        z = lambda *s: jnp.zeros(s, F32)
        st_s = (
            jnp.concatenate([z(1, 1, RWKV_COLS), state_shift[l]], axis=0),
            jnp.concatenate([z(1, N_HEADS, HEAD_DIM, HEAD_DIM), state_wkv[l]], axis=0),
            jnp.concatenate([z(1, CONV_W - 1, CONV_CH), cache_conv[l]], axis=0),
            jnp.concatenate([z(1, FFN_CONV_W - 1, D_FF), cache_ffn_conv[l]], axis=0),
        )
        ys, sh_s, wk_s, cv_s, ff_s = _layer(xs, t_s, *st_s, wts, (nb * t_pad, t_pad, t_pad, nb * t_pad))
        bc = lambda a: jnp.broadcast_to(a[0:1], (bp,) + a.shape[1:])
        yp, sh_p, wk_p, cv_p, ff_p = _layer(xp, t_prompt, bc(sh_s), bc(wk_s), bc(cv_s), bc(ff_s), wts,
                                            _prompt_tiles(bp, t_prompt))
        outs_s.append((sh_s[1:], wk_s[1:], cv_s[1:], ff_s[1:]))
        outs_p.append((sh_p, wk_p, cv_p, ff_p))
        assert depth == 1
        xs, xp = ys, yp

    stack = lambda outs, i: jnp.stack([o[i] for o in outs]).astype(dt)
    y_prompt = xp
    y_sample = xs[1:, :t_s]
    return (y_prompt, y_sample,
            stack(outs_p, 0), stack(outs_p, 1), stack(outs_p, 2), stack(outs_p, 3),
            stack(outs_s, 0), stack(outs_s, 1), stack(outs_s, 2), stack(outs_s, 3))
```

```python
import functools
import math

import jax
import jax.numpy as jnp
from jax import lax
from jax.experimental import pallas as pl
from jax.experimental.pallas import tpu as pltpu

F32 = jnp.float32
BF16 = jnp.bfloat16

D_MODEL = 2048
RWKV_DIM = 1024
HEAD_DIM = 64
N_HEADS = RWKV_DIM // HEAD_DIM
DECAY_LORA = 64
AAA_LORA = 64
GATE_LORA = 160
RWKV_COLS = 3 * RWKV_DIM + DECAY_LORA + AAA_LORA + GATE_LORA
CONV_CH = 1024
CONV_W = 31
D_FF = 5632
FFN_CONV_W = 3
N_META = 16
RMS_EPS = 1e-6
LN_EPS = 1e-5
GN_EPS = 1e-5 * HEAD_DIM

MXU_W = 256
SUBLANES = 8
VMEM_LIMIT_BYTES = 56 * 1024 * 1024

GROUP_W = MXU_W
GROUP_HEADS = GROUP_W // HEAD_DIM
N_GROUPS = RWKV_DIM // GROUP_W
CHUNK = MXU_W // GROUP_HEADS
N_DOUBLINGS = int(math.log2(CHUNK)) - 1

COL_TILE = 512
LORA_W = COL_TILE
LA_OFF = 3 * RWKV_DIM
LA_W = DECAY_LORA + AAA_LORA
GL_OFF = LA_OFF + MXU_W
GL_W = MXU_W
P_COLS = 3 * RWKV_DIM + LORA_W
N_RKV_TILES = 3 * RWKV_DIM // COL_TILE
N_P_TILES = P_COLS // COL_TILE
GLU_W = COL_TILE // 2
N_GLU_TILES = CONV_CH // GLU_W
CONV_HALO = 32
CONV_RB = 32
CONV_LANES = 256
FF_TILE = 512
N_FF_TILES = D_FF // FF_TILE


def _cparams(sem):
    return pltpu.CompilerParams(dimension_semantics=sem, vmem_limit_bytes=VMEM_LIMIT_BYTES)


def _dot(a, b):
    return jnp.dot(a, b, preferred_element_type=F32)


def _dot_nt(a, b):
    return lax.dot_general(a, b, (((1,), (1,)), ((), ())), preferred_element_type=F32)


def _dot_tn(a, b):
    return lax.dot_general(a, b, (((0,), (0,)), ((), ())), preferred_element_type=F32)


def _inproj_kernel(x_ref, g_ref, wrkv_ref, wlora_ref, wval_ref, wgate_ref, p_ref, c_ref, h_ref):
    j = pl.program_id(1)

    @pl.when(j == 0)
    def _():
        x = x_ref[...]
        ms = jnp.mean(x * x, axis=-1, keepdims=True)
        h_ref[...] = (x * lax.rsqrt(ms + RMS_EPS) * g_ref[...]).astype(BF16)

    @pl.when(j < N_RKV_TILES)
    def _():
        p_ref[...] = _dot(h_ref[...], wrkv_ref[...])

    @pl.when(j == N_RKV_TILES)
    def _():
        p_ref[...] = _dot(h_ref[...], wlora_ref[...])

    @pl.when(j >= N_P_TILES)
    def _():
        h = h_ref[...]
        c_ref[...] = _dot(h, wval_ref[...]) * jax.nn.sigmoid(_dot(h, wgate_ref[...]))


def _inproj(x2d, g, wts, tm):
    rows = x2d.shape[0]
    n_col = N_P_TILES + N_GLU_TILES
    glu_tile = lambda i, j: (0, jnp.maximum(j - N_P_TILES, 0))
    return pl.pallas_call(
        _inproj_kernel,
        grid=(rows // tm, n_col),
        in_specs=[
            pl.BlockSpec((tm, D_MODEL), lambda i, j: (i, 0)),
            pl.BlockSpec((1, D_MODEL), lambda i, j: (0, 0)),
            pl.BlockSpec((D_MODEL, COL_TILE), lambda i, j: (0, jnp.minimum(j, N_RKV_TILES - 1))),
            pl.BlockSpec((D_MODEL, LORA_W), lambda i, j: (0, 0)),
            pl.BlockSpec((D_MODEL, GLU_W), glu_tile),
            pl.BlockSpec((D_MODEL, GLU_W), glu_tile),
        ],
        out_specs=[
            pl.BlockSpec((tm, COL_TILE), lambda i, j: (i, jnp.minimum(j, N_P_TILES - 1))),
            pl.BlockSpec((tm, GLU_W), lambda i, j: (i, jnp.maximum(j - N_P_TILES, 0))),
        ],
        out_shape=[
            jax.ShapeDtypeStruct((rows, P_COLS), F32),
            jax.ShapeDtypeStruct((rows, CONV_CH), F32),
        ],
        scratch_shapes=[pltpu.VMEM((tm, D_MODEL), BF16)],
        compiler_params=_cparams(("arbitrary", "arbitrary")),
        name="inproj",
    )(x2d, g, wts["w_rkv"], wts["w_lora"], wts["w_val"], wts["w_gate"])


def _wkv_kernel(p_ref, shift0_ref, s0_ref, mu_ref, w0_ref, wdec_ref, a0_ref, waaa_ref, wgate_ref,
                kk_ref, ka_ref, rk_ref, lng_ref, lnb_ref, tri_ref, hmask_ref,
                y_ref, sout_ref,
                s_ref, prev_ref, rt_ref, at_ref, kt_ref, bt_ref, kh_ref, bh_ref, vv_ref, wa_ref, rp_ref, el_ref,
                yacc_ref, bonus_ref, gate_ref, u0_ref, *, tt, t_valid, t_total):
    t = pl.program_id(1)

    @pl.when(t == 0)
    def _():
        prev_ref[...] = shift0_ref[...]
        zero = jnp.zeros((HEAD_DIM, HEAD_DIM), F32)
        for g in range(N_GROUPS):
            band = [jnp.concatenate([s0_ref[g * GROUP_HEADS + h] if hh == h else zero for hh in range(GROUP_HEADS)],
                                    axis=1) for h in range(GROUP_HEADS)]
            s_ref[g] = jnp.concatenate(band, axis=0)

    hmask_bf = hmask_ref[...]
    head_shift = HEAD_DIM.bit_length() - 1
    hmask = (jnp.right_shift(lax.broadcasted_iota(jnp.int32, (GROUP_W, GROUP_W), 0), head_shift)
             == jnp.right_shift(lax.broadcasted_iota(jnp.int32, (GROUP_W, GROUP_W), 1), head_shift))

    def headsum(x):
        parts = [_dot(x[:, g * GROUP_W:(g + 1) * GROUP_W].astype(BF16), hmask_bf) for g in range(N_GROUPS)]
        return jnp.concatenate(parts, axis=1)

    p = p_ref[...]
    row = lax.broadcasted_iota(jnp.int32, (tt, 1), 0)
    p_prev = jnp.where(row == 0, prev_ref[...], pltpu.roll(p, 1, 0))
    prev_ref[...] = p[tt - 1:tt, :]
    ps = p + (p_prev - p) * mu_ref[...]
    r = ps[:, 0:RWKV_DIM]
    k = ps[:, RWKV_DIM:2 * RWKV_DIM]
    v = ps[:, 2 * RWKV_DIM:3 * RWKV_DIM]
    la = ps[:, LA_OFF:LA_OFF + LA_W]
    gl = ps[:, GL_OFF:GL_OFF + GL_W]

    z = w0_ref[...] + _dot(jnp.tanh(la).astype(BF16), wdec_ref[...])
    lw = -math.exp(-0.5) * jax.nn.sigmoid(z)
    a_lr = jax.nn.sigmoid(a0_ref[...] + _dot(la.astype(BF16), waaa_ref[...]))
    gate_ref[...] = _dot(jax.nn.sigmoid(gl).astype(BF16), wgate_ref[...])

    kk = k * kk_ref[...]
    kkn = kk * lax.rsqrt(jnp.maximum(headsum(kk * kk), 1e-24))
    k2 = k * (1.0 + (a_lr - 1.0) * ka_ref[...])
    bvec = kkn * a_lr
    avec = -kkn
    bonus_ref[...] = headsum(r * k2 * rk_ref[...]) * v

    if t_valid < t_total:
        valid = (t * tt + row) < t_valid
        lw = jnp.where(valid, lw, 0.0)
        k2 = jnp.where(valid, k2, 0.0)
        bvec = jnp.where(valid, bvec, 0.0)
        avec = jnp.where(valid, avec, 0.0)

    lw_hi = lw.astype(BF16)
    lw_lo = (lw - lw_hi.astype(F32)).astype(BF16)
    tri = tri_ref[...]
    cs = _dot(tri, lw_hi) + _dot(tri, lw_lo)
    ec = jnp.exp(cs)
    emc = jnp.exp(-cs)
    kt = k2 * emc
    bt = bvec * emc
    rt_ref[...] = (r * ec).astype(BF16)
    at_ref[...] = (avec * jnp.exp(cs - lw)).astype(BF16)
    kt_ref[...] = kt.astype(BF16)
    bt_ref[...] = bt.astype(BF16)
    vv_ref[...] = v.astype(BF16)
    for c in range(tt // CHUNK):
        rows = slice(c * CHUNK, (c + 1) * CHUNK)
        el = ec[(c + 1) * CHUNK - 1:(c + 1) * CHUNK, :]
        el_ref[c:c + 1, :] = el
        kh_ref[rows, :] = (kt[rows, :] * el).astype(BF16)
        bh_ref[rows, :] = (bt[rows, :] * el).astype(BF16)

    t_idx = lax.broadcasted_iota(jnp.int32, (CHUNK, GROUP_W), 0)
    s_idx = jnp.bitwise_and(lax.broadcasted_iota(jnp.int32, (CHUNK, GROUP_W), 1), CHUNK - 1)
    strict = t_idx > s_idx
    incl = t_idx >= s_idx
    eye = (t_idx == s_idx).astype(F32)

    def bdiag(zb):
        return jnp.concatenate([zb] * GROUP_HEADS, axis=0) * hmask_bf

    groups = range(N_GROUPS)
    cols = [slice(g * GROUP_W, (g + 1) * GROUP_W) for g in groups]
    rowcat = lambda a, b: jnp.concatenate([a, b], axis=0)

    for c in range(tt // CHUNK):
        rows = slice(c * CHUNK, (c + 1) * CHUNK)
        rt = [rt_ref[rows, cols[g]] for g in groups]
        at = [at_ref[rows, cols[g]] for g in groups]
        bd_v = [bdiag(vv_ref[rows, cols[g]]) for g in groups]
        lhs2 = [rowcat(at[g], rt[g]) for g in groups]
        gb = [_dot_nt(lhs2[g], bdiag(bt_ref[rows, cols[g]])) for g in groups]
        gk = [_dot_nt(lhs2[g], bdiag(kt_ref[rows, cols[g]])) for g in groups]
        a_ab = [jnp.where(strict, gb[g][:CHUNK], 0.0) for g in groups]
        a_rb = [jnp.where(incl, gb[g][CHUNK:], 0.0).astype(BF16) for g in groups]
        a_kk = [rowcat(jnp.where(strict, gk[g][:CHUNK], 0.0), jnp.where(incl, gk[g][CHUNK:], 0.0)).astype(BF16)
                for g in groups]
        pw_b = [a_ab[g].astype(BF16) for g in groups]
        tinv = [eye + a_ab[g] for g in groups]
        pw_b = [_dot(pw_b[g], bdiag(pw_b[g])).astype(BF16) for g in groups]
        for _ in range(N_DOUBLINGS - 1):
            res = [_dot(rowcat(pw_b[g], tinv[g].astype(BF16)), bdiag(pw_b[g])) for g in groups]
            pw_b = [res[g][:CHUNK].astype(BF16) for g in groups]
            tinv = [tinv[g] + res[g][CHUNK:] for g in groups]
        t_b = [(tinv[g] + _dot(tinv[g].astype(BF16), bdiag(pw_b[g]))).astype(BF16) for g in groups]
        av2 = [_dot(a_kk[g], bd_v[g]) for g in groups]
        w_a = [_dot(t_b[g], bdiag(at[g])).astype(BF16) for g in groups]
        u0 = [_dot(t_b[g], bdiag(av2[g][:CHUNK].astype(BF16))) for g in groups]
        for g in groups:
            wa_ref[rows, cols[g]] = w_a[g]
            u0_ref[rows, cols[g]] = u0[g]
            rp_ref[rows, cols[g]] = (rt[g].astype(F32) + _dot(a_rb[g], bdiag(w_a[g]))).astype(BF16)
            yacc_ref[rows, cols[g]] = _dot(a_rb[g], bdiag(u0[g].astype(BF16))) + av2[g][CHUNK:]

    for c in range(tt // CHUNK):
        rows = slice(c * CHUNK, (c + 1) * CHUNK)
        s_old = [s_ref[g] for g in groups]
        res = [_dot_nt(rowcat(wa_ref[rows, cols[g]], rp_ref[rows, cols[g]]), s_old[g].astype(BF16)) for g in groups]
        u_b = [(res[g][:CHUNK] + u0_ref[rows, cols[g]]).astype(BF16) for g in groups]
        upd = [_dot_tn(rowcat(u_b[g], vv_ref[rows, cols[g]]), rowcat(bh_ref[rows, cols[g]], kh_ref[rows, cols[g]]))
               for g in groups]
        for g in groups:
            yacc_ref[rows, cols[g]] = res[g][CHUNK:] + yacc_ref[rows, cols[g]]
            w_l = el_ref[c:c + 1, cols[g]]
            s_ref[g] = s_old[g] * w_l + jnp.where(hmask, upd[g], 0.0)

    y = yacc_ref[...]
    mean = headsum(y) * (1.0 / HEAD_DIM)
    d = y - mean
    var = headsum(d * d) * (1.0 / HEAD_DIM)
    yn = d * lax.rsqrt(var + GN_EPS) * lng_ref[...] + lnb_ref[...]
    y_ref[...] = ((yn + bonus_ref[...]) * gate_ref[...]).astype(BF16)

    @pl.when(t == pl.num_programs(1) - 1)
    def _():
        for g in range(N_GROUPS):
            s_g = s_ref[g]
            for h in range(GROUP_HEADS):
                lo = h * HEAD_DIM
                sout_ref[g * GROUP_HEADS + h] = s_g[lo:lo + HEAD_DIM, lo:lo + HEAD_DIM]


def _wkv(p3, shift0, s0, wts, tt, t_valid):
    b, t_total, _ = p3.shape
    idx = jnp.arange(tt)
    same_chunk = (idx[:, None] // CHUNK) == (idx[None, :] // CHUNK)
    tri = (same_chunk & (idx[None, :] <= idx[:, None])).astype(BF16)
    gidx = jnp.arange(GROUP_W) // HEAD_DIM
    hmask = (gidx[:, None] == gidx[None, :]).astype(BF16)

    def const(shape):
        nd = len(shape)
        return pl.BlockSpec(shape, lambda i, j, _nd=nd: (0,) * _nd)

    row = lambda w: const((1, w))
    kern = functools.partial(_wkv_kernel, tt=tt, t_valid=t_valid, t_total=t_total)
    act = lambda dt: pltpu.VMEM((tt, RWKV_DIM), dt)
    return pl.pallas_call(
        kern,
        grid=(b, t_total // tt),
        in_specs=[
            pl.BlockSpec((None, tt, P_COLS), lambda i, j: (i, j, 0)),
            pl.BlockSpec((None, 1, P_COLS), lambda i, j: (i, 0, 0)),
            pl.BlockSpec((None, N_HEADS, HEAD_DIM, HEAD_DIM), lambda i, j: (i, 0, 0, 0)),
            row(P_COLS), row(RWKV_DIM), const((LA_W, RWKV_DIM)), row(RWKV_DIM), const((LA_W, RWKV_DIM)),
            const((GL_W, RWKV_DIM)), row(RWKV_DIM), row(RWKV_DIM), row(RWKV_DIM), row(RWKV_DIM), row(RWKV_DIM),
            const((tt, tt)), const((GROUP_W, GROUP_W)),
        ],
        out_specs=[
            pl.BlockSpec((None, tt, RWKV_DIM), lambda i, j: (i, j, 0)),
            pl.BlockSpec((None, N_HEADS, HEAD_DIM, HEAD_DIM), lambda i, j: (i, 0, 0, 0)),
        ],
        out_shape=[
            jax.ShapeDtypeStruct((b, t_total, RWKV_DIM), BF16),
            jax.ShapeDtypeStruct((b, N_HEADS, HEAD_DIM, HEAD_DIM), F32),
        ],
        scratch_shapes=[
            pltpu.VMEM((N_GROUPS, GROUP_W, GROUP_W), F32),
            pltpu.VMEM((1, P_COLS), F32),
            act(BF16), act(BF16), act(BF16), act(BF16), act(BF16), act(BF16), act(BF16), act(BF16), act(BF16),
            pltpu.VMEM((max(tt // CHUNK, SUBLANES), RWKV_DIM), F32),
            act(F32), act(F32), act(F32), act(F32),
        ],
        compiler_params=_cparams(("arbitrary", "arbitrary")),
        name="wkv",
    )(p3, shift0, s0, wts["mu"], wts["w0"], wts["wdec"], wts["a0"], wts["waaa"], wts["wgate"],
      wts["k_k"], wts["k_a"], wts["r_k"], wts["lnx_g"], wts["lnx_b"], tri, hmask)


def _mixout_kernel(y_ref, c_ref, cache_ref, cw_ref, cb_ref, lg_ref, lb_ref, wo_ref, x_ref, o_ref,
                   cbuf_ref, cact_ref, *, tm, tiles_per_seq, n_tiles):
    s = pl.program_id(0)
    slot = lax.rem(s, 2)
    t_conv = lax.rem(jnp.minimum(s, n_tiles - 1), tiles_per_seq)

    @pl.when(s == 0)
    def _():
        cact_ref[1] = jnp.zeros((tm, CONV_CH), BF16)

    @pl.when(t_conv == 0)
    def _():
        cbuf_ref[0:CONV_HALO, :] = cache_ref[...]

    @pl.when(t_conv > 0)
    def _():
        cbuf_ref[0:CONV_HALO, :] = cbuf_ref[tm:tm + CONV_HALO, :]

    mix = (_dot(y_ref[...], wo_ref[0:RWKV_DIM, :])
           + _dot(cact_ref[1 - slot], wo_ref[RWKV_DIM:RWKV_DIM + CONV_CH, :]))
    o_ref[...] = x_ref[...] + mix

    cbuf_ref[CONV_HALO:CONV_HALO + tm, :] = c_ref[...]
    lead = CONV_HALO - (CONV_W - 1)
    n_win = CONV_RB + CONV_HALO
    rep = lambda a: jnp.concatenate([a] * (CONV_RB // SUBLANES), axis=0)
    bias = jnp.broadcast_to(cb_ref[...], (CONV_RB, CONV_CH))
    gain = jnp.broadcast_to(lg_ref[...], (CONV_RB, CONV_CH))
    shift = jnp.broadcast_to(lb_ref[...], (CONV_RB, CONV_CH))
    for rb in range(tm // CONV_RB):
        base = rb * CONV_RB
        parts = []
        for lc in range(CONV_CH // CONV_LANES):
            lanes = slice(lc * CONV_LANES, (lc + 1) * CONV_LANES)
            win = cbuf_ref[base:base + n_win, lanes]
            acc = None
            for sub in range(SUBLANES):
                taps = [j for j in range(CONV_W) if (lead + j) % SUBLANES == sub]
                if not taps:
                    continue
                shifted = win if sub == 0 else pltpu.roll(win, n_win - sub, 0)
                for j in taps:
                    off = lead + j - sub
                    term = rep(cw_ref[j, :, lanes]) * shifted[off:off + CONV_RB, :]
                    acc = term if acc is None else acc + term
            parts.append(acc)
        acc = jnp.concatenate(parts, axis=1) + bias
        mean = jnp.mean(acc, axis=-1, keepdims=True)
        d = acc - mean
        var = jnp.mean(d * d, axis=-1, keepdims=True)
        yn = d * lax.rsqrt(var + LN_EPS) * gain + shift
        cact_ref[slot, base:base + CONV_RB, :] = (yn * jax.nn.sigmoid(yn)).astype(BF16)


def _mixout(y3, c3, cache, wts, x3, tm):
    b, t_total, _ = x3.shape
    tps = t_total // tm
    n_tiles = b * tps
    conv_tile = lambda s: jnp.minimum(s, n_tiles - 1)
    proj_tile = lambda s: jnp.maximum(s - 1, 0)

    def const(shape):
        nd = len(shape)
        return pl.BlockSpec(shape, lambda s, _nd=nd: (0,) * _nd)

    def rows(width, tile_of):
        return pl.BlockSpec((None, tm, width), lambda s: (tile_of(s) // tps, tile_of(s) % tps, 0))

    return pl.pallas_call(
        functools.partial(_mixout_kernel, tm=tm, tiles_per_seq=tps, n_tiles=n_tiles),
        grid=(n_tiles + 1,),
        in_specs=[
            rows(RWKV_DIM, proj_tile),
            rows(CONV_CH, conv_tile),
            pl.BlockSpec((None, CONV_HALO, CONV_CH), lambda s: (conv_tile(s) // tps, 0, 0)),
            const((CONV_HALO, SUBLANES, CONV_CH)), const((1, CONV_CH)), const((1, CONV_CH)), const((1, CONV_CH)),
            const((RWKV_DIM + CONV_CH, D_MODEL)),
            rows(D_MODEL, proj_tile),
        ],
        out_specs=rows(D_MODEL, proj_tile),
        out_shape=jax.ShapeDtypeStruct((b, t_total, D_MODEL), F32),
        scratch_shapes=[
            pltpu.VMEM((tm + CONV_HALO, CONV_CH), F32),
            pltpu.VMEM((2, tm, CONV_CH), BF16),
        ],
        compiler_params=_cparams(("arbitrary",)),
        name="mixout",
    )(y3, c3, cache, wts["conv_w"], wts["conv_b"], wts["conv_ln_g"], wts["conv_ln_b"], wts["w_out"], x3)


def _gelu_tanh(x):
    return 0.5 * x * (1.0 + jnp.tanh(math.sqrt(2.0 / math.pi) * (x + 0.044715 * (x * x * x))))


def _ffn_kernel(x_ref, gn_ref, wu_ref, wz_ref, fw_ref, fb_ref, wd_ref, cache_ref, fn_ref,
                o_ref, tail_ref, h_ref, acc_ref, *, tm, n_seq, seq_rows, tail_off):
    i = pl.program_id(1)
    j = pl.program_id(2)

    @pl.when((i == 0) & (j == 0))
    def _():
        tail_ref[...] = cache_ref[...]

    @pl.when(j == 0)
    def _():
        x = x_ref[...]
        ms = jnp.mean(x * x, axis=-1, keepdims=True)
        h_ref[...] = (x * lax.rsqrt(ms + RMS_EPS) * gn_ref[...]).astype(BF16)
        acc_ref[...] = jnp.zeros_like(acc_ref)

    h = h_ref[...]
    u = _dot(h, wu_ref[...])
    z = _dot(h, wz_ref[...])
    prev = tail_ref[j]
    row = lax.broadcasted_iota(jnp.int32, (tm, 1), 0)
    u_m1 = pltpu.roll(u, 1, 0)
    u_m2 = pltpu.roll(u, 2, 0)
    for s in range(n_seq):
        pm1 = prev[(s + 1) * SUBLANES - 1:(s + 1) * SUBLANES, :]
        pm2 = prev[(s + 1) * SUBLANES - 2:(s + 1) * SUBLANES - 1, :]
        u_m1 = jnp.where(row == s * seq_rows, pm1, u_m1)
        u_m2 = jnp.where(row == s * seq_rows, pm2, jnp.where(row == s * seq_rows + 1, pm1, u_m2))
    uc = fw_ref[0:1, :] * u_m2 + fw_ref[1:2, :] * u_m1 + fw_ref[2:3, :] * u + fb_ref[...]

    for s in range(n_seq):
        end = s * seq_rows + tail_off
        tail_ref[j, s * SUBLANES:(s + 1) * SUBLANES, :] = u[end - SUBLANES:end, :]

    act = (_gelu_tanh(uc) * z).astype(BF16)
    acc_ref[...] += _dot(act, wd_ref[...])

    @pl.when(j == pl.num_programs(2) - 1)
    def _():
        x2 = x_ref[...] + acc_ref[...]
        ms = jnp.mean(x2 * x2, axis=-1, keepdims=True)
        o_ref[...] = x2 * lax.rsqrt(ms + RMS_EPS) * fn_ref[...]


def _ffn(x3, cache, wts, tm, t_valid):
    b0, t0, _ = x3.shape
    n_seq = max(tm // t0, 1)
    assert b0 % n_seq == 0 and (tm % t0 == 0 or t0 % tm == 0)
    b, t_total = b0 // n_seq, t0 * n_seq
    x3 = x3.reshape(b, t_total, D_MODEL)
    n_i = t_total // tm
    assert t_valid == t0 or n_i == 1
    tail_off = t_valid if n_i == 1 else tm
    assert tail_off >= SUBLANES
    tail_rows = n_seq * SUBLANES
    cache = jnp.concatenate([jnp.zeros((b0, SUBLANES - (FFN_CONV_W - 1), D_FF), F32), cache.astype(F32)], axis=1)
    cache = cache.reshape(b, tail_rows, N_FF_TILES, FF_TILE).transpose(0, 2, 1, 3)

    def const(shape):
        nd = len(shape)
        return pl.BlockSpec(shape, lambda s, i, j, _nd=nd: (0,) * _nd)

    kern = functools.partial(_ffn_kernel, tm=tm, n_seq=n_seq, seq_rows=t0, tail_off=tail_off)
    y, tail = pl.pallas_call(
        kern,
        grid=(b, n_i, N_FF_TILES),
        in_specs=[
            pl.BlockSpec((None, tm, D_MODEL), lambda s, i, j: (s, i, 0)),
            const((1, D_MODEL)),
            pl.BlockSpec((D_MODEL, FF_TILE), lambda s, i, j: (0, j)),
            pl.BlockSpec((D_MODEL, FF_TILE), lambda s, i, j: (0, j + N_FF_TILES)),
            pl.BlockSpec((SUBLANES, FF_TILE), lambda s, i, j: (0, j)),
            pl.BlockSpec((1, FF_TILE), lambda s, i, j: (0, j)),
            pl.BlockSpec((FF_TILE, D_MODEL), lambda s, i, j: (j, 0)),
            pl.BlockSpec((None, N_FF_TILES, tail_rows, FF_TILE), lambda s, i, j: (s, 0, 0, 0)),
            const((1, D_MODEL)),
        ],
        out_specs=[
            pl.BlockSpec((None, tm, D_MODEL), lambda s, i, j: (s, i, 0)),
            pl.BlockSpec((None, N_FF_TILES, tail_rows, FF_TILE), lambda s, i, j: (s, 0, 0, 0)),
        ],
        out_shape=[
            jax.ShapeDtypeStruct((b, t_total, D_MODEL), F32),
            jax.ShapeDtypeStruct((b, N_FF_TILES, tail_rows, FF_TILE), F32),
        ],
        scratch_shapes=[
            pltpu.VMEM((tm, D_MODEL), BF16),
            pltpu.VMEM((tm, D_MODEL), F32),
        ],
        compiler_params=_cparams(("arbitrary", "arbitrary", "arbitrary")),
        name="ffn",
    )(x3, wts["norm_ffn"], wts["w_up"], wts["w_up"], wts["ffn_conv_w"], wts["ffn_conv_b"],
      wts["w_down"], cache, wts["final_norm"])
    new_cache = tail.transpose(0, 2, 1, 3).reshape(b0, SUBLANES, D_FF)[:, SUBLANES - (FFN_CONV_W - 1):]
    return y.reshape(b0, t0, D_MODEL), new_cache


def _pad_cols_rwkv(a):
    d3 = 3 * RWKV_DIM
    lead = a.shape[:-1]
    zeros = lambda w: jnp.zeros(lead + (w,), a.dtype)
    return jnp.concatenate([
        a[..., :d3 + LA_W], zeros(GL_OFF - LA_OFF - LA_W),
        a[..., d3 + LA_W:], zeros(GL_W - GATE_LORA)], axis=-1)


def _unpad_cols_rwkv(a):
    d3 = 3 * RWKV_DIM
    return jnp.concatenate([a[..., :d3 + LA_W], a[..., GL_OFF:GL_OFF + GATE_LORA]], axis=-1)


def _prep_weights(norm_mix, w_in, tshift_mu, w0, w_decay_up, a0, w_aaa_up, w_gate_up, k_k, k_a, r_k,
                  lnx_g, lnx_b, conv_w, conv_b, conv_ln_g, conv_ln_b, w_out, norm_ffn, w_ffn_up,
                  ffn_conv_w, ffn_conv_b, w_ffn_down, final_norm):
    row = lambda a: a.reshape(1, -1).astype(F32)
    w_in_b = w_in.astype(BF16)
    zrows = lambda n: jnp.zeros((n, RWKV_DIM), F32)
    return {
        "norm_mix": row(norm_mix),
        "w_rkv": w_in_b[:, :3 * RWKV_DIM],
        "w_lora": _pad_cols_rwkv(w_in_b[:, :RWKV_COLS])[:, 3 * RWKV_DIM:],
        "w_val": w_in_b[:, RWKV_COLS:RWKV_COLS + CONV_CH],
        "w_gate": w_in_b[:, RWKV_COLS + CONV_CH:],
        "mu": row(_pad_cols_rwkv(tshift_mu)),
        "w0": row(w0),
        "wdec": jnp.concatenate([w_decay_up, zrows(AAA_LORA)], axis=0).astype(BF16),
        "a0": row(a0),
        "waaa": jnp.concatenate([zrows(DECAY_LORA), w_aaa_up], axis=0).astype(BF16),
        "wgate": jnp.concatenate([w_gate_up, zrows(GL_W - GATE_LORA)], axis=0).astype(BF16),
        "k_k": row(k_k), "k_a": row(k_a), "r_k": row(r_k), "lnx_g": row(lnx_g), "lnx_b": row(lnx_b),
        "conv_w": jnp.broadcast_to(
            jnp.concatenate([conv_w, jnp.zeros((CONV_HALO - CONV_W, CONV_CH), F32)], axis=0).astype(F32)[:, None, :],
            (CONV_HALO, SUBLANES, CONV_CH)),
        "conv_b": row(conv_b), "conv_ln_g": row(conv_ln_g), "conv_ln_b": row(conv_ln_b),
        "w_out": w_out.astype(BF16),
        "norm_ffn": row(norm_ffn),
        "w_up": w_ffn_up.astype(BF16),
        "ffn_conv_w": jnp.concatenate([ffn_conv_w, jnp.zeros((SUBLANES - FFN_CONV_W, D_FF), F32)], axis=0).astype(F32),
        "ffn_conv_b": row(ffn_conv_b),
        "w_down": w_ffn_down.astype(BF16),
        "final_norm": row(final_norm),
    }


def _layer(x3, t_valid, shift, wkv_state, conv_cache, ffn_cache, wts, tiles):
    b, t_total, _ = x3.shape
    tm_in, tt, tm_mix, tm_ffn = tiles
    p2, c2 = _inproj(x3.reshape(b * t_total, D_MODEL), wts["norm_mix"], wts, tm_in)
    p3 = p2.reshape(b, t_total, P_COLS)
    c3 = c2.reshape(b, t_total, CONV_CH)

    y3, new_wkv = _wkv(p3, _pad_cols_rwkv(shift.astype(F32)), wkv_state.astype(F32), wts, tt, t_valid)

    cache_pad = jnp.concatenate(
        [jnp.zeros((b, CONV_HALO - (CONV_W - 1), CONV_CH), F32), conv_cache.astype(F32)], axis=1)
    x1 = _mixout(y3, c3, cache_pad, wts, x3, tm_mix)

    y, new_ffn = _ffn(x1, ffn_cache, wts, tm_ffn, t_valid)

    new_shift = _unpad_cols_rwkv(p3[:, t_valid - 1:t_valid, :])
    hist = CONV_W - 1
    if t_valid >= hist:
        new_conv = c3[:, t_valid - hist:t_valid]
    else:
        new_conv = jnp.concatenate([conv_cache.astype(F32)[:, t_valid:], c3[:, :t_valid]], axis=1)
    return y, new_shift, new_wkv, new_conv, new_ffn


def _prompt_tiles(b, t):
    tiles = (min(1024, b * t), min(256, t), min(512, t), min(512, t))
    assert (b * t) % tiles[0] == 0 and all(t % x == 0 for x in tiles[1:]) and tiles[1] % CHUNK == 0
    return tiles


def kernel(x_prompt, x_sample, state_shift, state_wkv, cache_conv, cache_ffn_conv, meta, norm_mix, w_in,
           tshift_mu, w0, w_decay_up, a0, w_aaa_up, w_gate_up, k_k, k_a, r_k, lnx_g, lnx_b, conv_w, conv_b,
           conv_ln_g, conv_ln_b, w_out, norm_ffn, w_ffn_up, ffn_conv_w, ffn_conv_b, w_ffn_down, final_norm):
    depth = w_in.shape[0]
    bp, t_prompt, _ = x_prompt.shape
    bs, t_s, _ = x_sample.shape
    assert t_s == N_META, "the short streams (meta prefix, sample) share one padded pass"
    dt = x_prompt.dtype

    t_pad = CHUNK
    xs = jnp.concatenate([meta.astype(dt)[None], x_sample], axis=0)
    xs = jnp.pad(xs, ((0, 0), (0, t_pad - t_s), (0, 0)))
    xp = x_prompt
    nb = bs + 1

    outs_p, outs_s = [], []
    for l in range(depth):
        wts = _prep_weights(norm_mix[l], w_in[l], tshift_mu[l], w0[l], w_decay_up[l], a0[l], w_aaa_up[l],
                            w_gate_up[l], k_k[l], k_a[l], r_k[l], lnx_g[l], lnx_b[l], conv_w[l], conv_b[l],
                            conv_ln_g[l], conv_ln_b[l], w_out[l], norm_ffn[l], w_ffn_up[l], ffn_conv_w[l],
                            ffn_conv_b[l], w_ffn_down[l], final_norm)
        z = lambda *s: jnp.zeros(s, F32)
        st_s = (
            jnp.concatenate([z(1, 1, RWKV_COLS), state_shift[l]], axis=0),
            jnp.concatenate([z(1, N_HEADS, HEAD_DIM, HEAD_DIM), state_wkv[l]], axis=0),
            jnp.concatenate([z(1, CONV_W - 1, CONV_CH), cache_conv[l]], axis=0),
            jnp.concatenate([z(1, FFN_CONV_W - 1, D_FF), cache_ffn_conv[l]], axis=0),
        )
        ys, sh_s, wk_s, cv_s, ff_s = _layer(xs, t_s, *st_s, wts, (nb * t_pad, t_pad, t_pad, nb * t_pad))
        bc = lambda a: jnp.broadcast_to(a[0:1], (bp,) + a.shape[1:])
        yp, sh_p, wk_p, cv_p, ff_p = _layer(xp, t_prompt, bc(sh_s), bc(wk_s), bc(cv_s), bc(ff_s), wts,
                                            _prompt_tiles(bp, t_prompt))
        outs_s.append((sh_s[1:], wk_s[1:], cv_s[1:], ff_s[1:]))
        outs_p.append((sh_p, wk_p, cv_p, ff_p))
        assert depth == 1
        xs, xp = ys, yp

    stack = lambda outs, i: jnp.stack([o[i] for o in outs]).astype(dt)
    y_prompt = xp
    y_sample = xs[1:, :t_s]
    return (y_prompt, y_sample,
            stack(outs_p, 0), stack(outs_p, 1), stack(outs_p, 2), stack(outs_p, 3),
            stack(outs_s, 0), stack(outs_s, 1), stack(outs_s, 2), stack(outs_s, 3))
```

```python
import functools
import math

import jax
import jax.numpy as jnp
from jax import lax
from jax.experimental import pallas as pl
from jax.experimental.pallas import tpu as pltpu

F32 = jnp.float32
BF16 = jnp.bfloat16

D_MODEL = 2048
RWKV_DIM = 1024
HEAD_DIM = 64
N_HEADS = RWKV_DIM // HEAD_DIM
DECAY_LORA = 64
AAA_LORA = 64
GATE_LORA = 160
RWKV_COLS = 3 * RWKV_DIM + DECAY_LORA + AAA_LORA + GATE_LORA
CONV_CH = 1024
CONV_W = 31
D_FF = 5632
FFN_CONV_W = 3
N_META = 16
RMS_EPS = 1e-6
LN_EPS = 1e-5
GN_EPS = 1e-5 * HEAD_DIM

MXU_W = 256
SUBLANES = 8
VMEM_LIMIT_BYTES = 56 * 1024 * 1024

GROUP_W = MXU_W
GROUP_HEADS = GROUP_W // HEAD_DIM
N_GROUPS = RWKV_DIM // GROUP_W
CHUNK = MXU_W // GROUP_HEADS
N_DOUBLINGS = int(math.log2(CHUNK)) - 1

COL_TILE = 512
LORA_W = COL_TILE
LA_OFF = 3 * RWKV_DIM
LA_W = DECAY_LORA + AAA_LORA
GL_OFF = LA_OFF + MXU_W
GL_W = MXU_W
P_COLS = 3 * RWKV_DIM + LORA_W
N_RKV_TILES = 3 * RWKV_DIM // COL_TILE
N_P_TILES = P_COLS // COL_TILE
GLU_W = COL_TILE // 2
N_GLU_TILES = CONV_CH // GLU_W
CONV_HALO = 32
CONV_RB = 32
CONV_LANES = 256
FF_TILE = 512
N_FF_TILES = D_FF // FF_TILE


def _cparams(sem):
    return pltpu.CompilerParams(dimension_semantics=sem, vmem_limit_bytes=VMEM_LIMIT_BYTES)


def _dot(a, b):
    return jnp.dot(a, b, preferred_element_type=F32)


def _dot_nt(a, b):
    return lax.dot_general(a, b, (((1,), (1,)), ((), ())), preferred_element_type=F32)


def _dot_tn(a, b):
    return lax.dot_general(a, b, (((0,), (0,)), ((), ())), preferred_element_type=F32)


def _inproj_kernel(x_ref, g_ref, wrkv_ref, wlora_ref, wval_ref, wgate_ref, p_ref, c_ref, h_ref):
    j = pl.program_id(1)

    @pl.when(j == 0)
    def _():
        x = x_ref[...]
        ms = jnp.mean(x * x, axis=-1, keepdims=True)
        h_ref[...] = (x * lax.rsqrt(ms + RMS_EPS) * g_ref[...]).astype(BF16)

    @pl.when(j < N_RKV_TILES)
    def _():
        p_ref[...] = _dot(h_ref[...], wrkv_ref[...])

    @pl.when(j == N_RKV_TILES)
    def _():
        p_ref[...] = _dot(h_ref[...], wlora_ref[...])

    @pl.when(j >= N_P_TILES)
    def _():
        h = h_ref[...]
        c_ref[...] = _dot(h, wval_ref[...]) * jax.nn.sigmoid(_dot(h, wgate_ref[...]))


def _inproj(x2d, g, wts, tm):
    rows = x2d.shape[0]
    n_col = N_P_TILES + N_GLU_TILES
    glu_tile = lambda i, j: (0, jnp.maximum(j - N_P_TILES, 0))
    return pl.pallas_call(
        _inproj_kernel,
        grid=(rows // tm, n_col),
        in_specs=[
            pl.BlockSpec((tm, D_MODEL), lambda i, j: (i, 0)),
            pl.BlockSpec((1, D_MODEL), lambda i, j: (0, 0)),
            pl.BlockSpec((D_MODEL, COL_TILE), lambda i, j: (0, jnp.minimum(j, N_RKV_TILES - 1))),
            pl.BlockSpec((D_MODEL, LORA_W), lambda i, j: (0, 0)),
            pl.BlockSpec((D_MODEL, GLU_W), glu_tile),
            pl.BlockSpec((D_MODEL, GLU_W), glu_tile),
        ],
        out_specs=[
            pl.BlockSpec((tm, COL_TILE), lambda i, j: (i, jnp.minimum(j, N_P_TILES - 1))),
            pl.BlockSpec((tm, GLU_W), lambda i, j: (i, jnp.maximum(j - N_P_TILES, 0))),
        ],
        out_shape=[
            jax.ShapeDtypeStruct((rows, P_COLS), F32),
            jax.ShapeDtypeStruct((rows, CONV_CH), F32),
        ],
        scratch_shapes=[pltpu.VMEM((tm, D_MODEL), BF16)],
        compiler_params=_cparams(("arbitrary", "arbitrary")),
        name="inproj",
    )(x2d, g, wts["w_rkv"], wts["w_lora"], wts["w_val"], wts["w_gate"])


def _wkv_kernel(p_ref, shift0_ref, s0_ref, mu_ref, w0_ref, wdec_ref, a0_ref, waaa_ref, wgate_ref,
                kk_ref, ka_ref, rk_ref, lng_ref, lnb_ref, tri_ref, hmask_ref,
                y_ref, sout_ref,
                s_ref, prev_ref, rt_ref, at_ref, kt_ref, bt_ref, kh_ref, bh_ref, vv_ref, wa_ref, rp_ref, el_ref,
                yacc_ref, bonus_ref, gate_ref, u0_ref, *, tt, t_valid, t_total):
    t = pl.program_id(1)

    @pl.when(t == 0)
    def _():
        prev_ref[...] = shift0_ref[...]
        zero = jnp.zeros((HEAD_DIM, HEAD_DIM), F32)
        for g in range(N_GROUPS):
            band = [jnp.concatenate([s0_ref[g * GROUP_HEADS + h] if hh == h else zero for hh in range(GROUP_HEADS)],
                                    axis=1) for h in range(GROUP_HEADS)]
            s_ref[g] = jnp.concatenate(band, axis=0)

    hmask_bf = hmask_ref[...]
    head_shift = HEAD_DIM.bit_length() - 1
    hmask = (jnp.right_shift(lax.broadcasted_iota(jnp.int32, (GROUP_W, GROUP_W), 0), head_shift)
             == jnp.right_shift(lax.broadcasted_iota(jnp.int32, (GROUP_W, GROUP_W), 1), head_shift))

    def headsum(x):
        parts = [_dot(x[:, g * GROUP_W:(g + 1) * GROUP_W].astype(BF16), hmask_bf) for g in range(N_GROUPS)]
        return jnp.concatenate(parts, axis=1)

    p = p_ref[...]
    row = lax.broadcasted_iota(jnp.int32, (tt, 1), 0)
    p_prev = jnp.where(row == 0, prev_ref[...], pltpu.roll(p, 1, 0))
    prev_ref[...] = p[tt - 1:tt, :]
    ps = p + (p_prev - p) * mu_ref[...]
    r = ps[:, 0:RWKV_DIM]
    k = ps[:, RWKV_DIM:2 * RWKV_DIM]
    v = ps[:, 2 * RWKV_DIM:3 * RWKV_DIM]
    la = ps[:, LA_OFF:LA_OFF + LA_W]
    gl = ps[:, GL_OFF:GL_OFF + GL_W]

    z = w0_ref[...] + _dot(jnp.tanh(la).astype(BF16), wdec_ref[...])
    lw = -math.exp(-0.5) * jax.nn.sigmoid(z)
    a_lr = jax.nn.sigmoid(a0_ref[...] + _dot(la.astype(BF16), waaa_ref[...]))
    gate_ref[...] = _dot(jax.nn.sigmoid(gl).astype(BF16), wgate_ref[...])

    kk = k * kk_ref[...]
    kkn = kk * lax.rsqrt(jnp.maximum(headsum(kk * kk), 1e-24))
    k2 = k * (1.0 + (a_lr - 1.0) * ka_ref[...])
    bvec = kkn * a_lr
    avec = -kkn
    bonus_ref[...] = headsum(r * k2 * rk_ref[...]) * v

    if t_valid < t_total:
        valid = (t * tt + row) < t_valid
        lw = jnp.where(valid, lw, 0.0)
        k2 = jnp.where(valid, k2, 0.0)
        bvec = jnp.where(valid, bvec, 0.0)
        avec = jnp.where(valid, avec, 0.0)

    lw_hi = lw.astype(BF16)
    lw_lo = (lw - lw_hi.astype(F32)).astype(BF16)
    tri = tri_ref[...]
    cs = _dot(tri, lw_hi) + _dot(tri, lw_lo)
    ec = jnp.exp(cs)
    emc = jnp.exp(-cs)
    kt = k2 * emc
    bt = bvec * emc
    rt_ref[...] = (r * ec).astype(BF16)
    at_ref[...] = (avec * jnp.exp(cs - lw)).astype(BF16)
    kt_ref[...] = kt.astype(BF16)
    bt_ref[...] = bt.astype(BF16)
    vv_ref[...] = v.astype(BF16)
    for c in range(tt // CHUNK):
        rows = slice(c * CHUNK, (c + 1) * CHUNK)
        el = ec[(c + 1) * CHUNK - 1:(c + 1) * CHUNK, :]
        el_ref[c:c + 1, :] = el
        kh_ref[rows, :] = (kt[rows, :] * el).astype(BF16)
        bh_ref[rows, :] = (bt[rows, :] * el).astype(BF16)

    t_idx = lax.broadcasted_iota(jnp.int32, (CHUNK, GROUP_W), 0)
    s_idx = jnp.bitwise_and(lax.broadcasted_iota(jnp.int32, (CHUNK, GROUP_W), 1), CHUNK - 1)
    strict = t_idx > s_idx
    incl = t_idx >= s_idx
    eye = (t_idx == s_idx).astype(F32)

    def bdiag(zb):
        return jnp.concatenate([zb] * GROUP_HEADS, axis=0) * hmask_bf

    groups = range(N_GROUPS)
    cols = [slice(g * GROUP_W, (g + 1) * GROUP_W) for g in groups]
    rowcat = lambda a, b: jnp.concatenate([a, b], axis=0)

    for c in range(tt // CHUNK):
        rows = slice(c * CHUNK, (c + 1) * CHUNK)
        rt = [rt_ref[rows, cols[g]] for g in groups]
        at = [at_ref[rows, cols[g]] for g in groups]
        bd_v = [bdiag(vv_ref[rows, cols[g]]) for g in groups]
        lhs2 = [rowcat(at[g], rt[g]) for g in groups]
        gb = [_dot_nt(lhs2[g], bdiag(bt_ref[rows, cols[g]])) for g in groups]
        gk = [_dot_nt(lhs2[g], bdiag(kt_ref[rows, cols[g]])) for g in groups]
        a_ab = [jnp.where(strict, gb[g][:CHUNK], 0.0) for g in groups]
        a_rb = [jnp.where(incl, gb[g][CHUNK:], 0.0).astype(BF16) for g in groups]
        a_kk = [rowcat(jnp.where(strict, gk[g][:CHUNK], 0.0), jnp.where(incl, gk[g][CHUNK:], 0.0)).astype(BF16)
                for g in groups]
        pw_b = [a_ab[g].astype(BF16) for g in groups]
        tinv = [eye + a_ab[g] for g in groups]
        pw_b = [_dot(pw_b[g], bdiag(pw_b[g])).astype(BF16) for g in groups]
        for _ in range(N_DOUBLINGS - 1):
            res = [_dot(rowcat(pw_b[g], tinv[g].astype(BF16)), bdiag(pw_b[g])) for g in groups]
            pw_b = [res[g][:CHUNK].astype(BF16) for g in groups]
            tinv = [tinv[g] + res[g][CHUNK:] for g in groups]
        t_b = [(tinv[g] + _dot(tinv[g].astype(BF16), bdiag(pw_b[g]))).astype(BF16) for g in groups]
        av2 = [_dot(a_kk[g], bd_v[g]) for g in groups]
        w_a = [_dot(t_b[g], bdiag(at[g])).astype(BF16) for g in groups]
        u0 = [_dot(t_b[g], bdiag(av2[g][:CHUNK].astype(BF16))) for g in groups]
        for g in groups:
            wa_ref[rows, cols[g]] = w_a[g]
            u0_ref[rows, cols[g]] = u0[g]
            rp_ref[rows, cols[g]] = (rt[g].astype(F32) + _dot(a_rb[g], bdiag(w_a[g]))).astype(BF16)
            yacc_ref[rows, cols[g]] = _dot(a_rb[g], bdiag(u0[g].astype(BF16))) + av2[g][CHUNK:]

    for c in range(tt // CHUNK):
        rows = slice(c * CHUNK, (c + 1) * CHUNK)
        s_old = [s_ref[g] for g in groups]
        res = [_dot_nt(rowcat(wa_ref[rows, cols[g]], rp_ref[rows, cols[g]]), s_old[g].astype(BF16)) for g in groups]
        u_b = [(res[g][:CHUNK] + u0_ref[rows, cols[g]]).astype(BF16) for g in groups]
        upd = [_dot_tn(rowcat(u_b[g], vv_ref[rows, cols[g]]), rowcat(bh_ref[rows, cols[g]], kh_ref[rows, cols[g]]))
               for g in groups]
        for g in groups:
            yacc_ref[rows, cols[g]] = res[g][CHUNK:] + yacc_ref[rows, cols[g]]
            w_l = el_ref[c:c + 1, cols[g]]
            s_ref[g] = s_old[g] * w_l + jnp.where(hmask, upd[g], 0.0)

    y = yacc_ref[...]
    mean = headsum(y) * (1.0 / HEAD_DIM)
    d = y - mean
    var = headsum(d * d) * (1.0 / HEAD_DIM)
    yn = d * lax.rsqrt(var + GN_EPS) * lng_ref[...] + lnb_ref[...]
    y_ref[...] = ((yn + bonus_ref[...]) * gate_ref[...]).astype(BF16)

    @pl.when(t == pl.num_programs(1) - 1)
    def _():
        for g in range(N_GROUPS):
            s_g = s_ref[g]
            for h in range(GROUP_HEADS):
                lo = h * HEAD_DIM
                sout_ref[g * GROUP_HEADS + h] = s_g[lo:lo + HEAD_DIM, lo:lo + HEAD_DIM]


def _wkv(p3, shift0, s0, wts, tt, t_valid):
    b, t_total, _ = p3.shape
    idx = jnp.arange(tt)
    same_chunk = (idx[:, None] // CHUNK) == (idx[None, :] // CHUNK)
    tri = (same_chunk & (idx[None, :] <= idx[:, None])).astype(BF16)
    gidx = jnp.arange(GROUP_W) // HEAD_DIM
    hmask = (gidx[:, None] == gidx[None, :]).astype(BF16)

    def const(shape):
        nd = len(shape)
        return pl.BlockSpec(shape, lambda i, j, _nd=nd: (0,) * _nd)

    row = lambda w: const((1, w))
    kern = functools.partial(_wkv_kernel, tt=tt, t_valid=t_valid, t_total=t_total)
    act = lambda dt: pltpu.VMEM((tt, RWKV_DIM), dt)
    return pl.pallas_call(
        kern,
        grid=(b, t_total // tt),
        in_specs=[
            pl.BlockSpec((None, tt, P_COLS), lambda i, j: (i, j, 0)),
            pl.BlockSpec((None, 1, P_COLS), lambda i, j: (i, 0, 0)),
            pl.BlockSpec((None, N_HEADS, HEAD_DIM, HEAD_DIM), lambda i, j: (i, 0, 0, 0)),
            row(P_COLS), row(RWKV_DIM), const((LA_W, RWKV_DIM)), row(RWKV_DIM), const((LA_W, RWKV_DIM)),
            const((GL_W, RWKV_DIM)), row(RWKV_DIM), row(RWKV_DIM), row(RWKV_DIM), row(RWKV_DIM), row(RWKV_DIM),
            const((tt, tt)), const((GROUP_W, GROUP_W)),
        ],
        out_specs=[
            pl.BlockSpec((None, tt, RWKV_DIM), lambda i, j: (i, j, 0)),
            pl.BlockSpec((None, N_HEADS, HEAD_DIM, HEAD_DIM), lambda i, j: (i, 0, 0, 0)),
        ],
        out_shape=[
            jax.ShapeDtypeStruct((b, t_total, RWKV_DIM), BF16),
            jax.ShapeDtypeStruct((b, N_HEADS, HEAD_DIM, HEAD_DIM), F32),
        ],
        scratch_shapes=[
            pltpu.VMEM((N_GROUPS, GROUP_W, GROUP_W), F32),
            pltpu.VMEM((1, P_COLS), F32),
            act(BF16), act(BF16), act(BF16), act(BF16), act(BF16), act(BF16), act(BF16), act(BF16), act(BF16),
            pltpu.VMEM((max(tt // CHUNK, SUBLANES), RWKV_DIM), F32),
            act(F32), act(F32), act(F32), act(F32),
        ],
        compiler_params=_cparams(("arbitrary", "arbitrary")),
        name="wkv",
    )(p3, shift0, s0, wts["mu"], wts["w0"], wts["wdec"], wts["a0"], wts["waaa"], wts["wgate"],
      wts["k_k"], wts["k_a"], wts["r_k"], wts["lnx_g"], wts["lnx_b"], tri, hmask)


def _mixout_kernel(y_ref, c_ref, cache_ref, cw_ref, cb_ref, lg_ref, lb_ref, wo_ref, x_ref, o_ref,
                   cbuf_ref, cact_ref, *, tm, tiles_per_seq, n_tiles):
    s = pl.program_id(0)
    slot = lax.rem(s, 2)
    t_conv = lax.rem(jnp.minimum(s, n_tiles - 1), tiles_per_seq)

    @pl.when(s == 0)
    def _():
        cact_ref[1] = jnp.zeros((tm, CONV_CH), BF16)

    @pl.when(t_conv == 0)
    def _():
        cbuf_ref[0:CONV_HALO, :] = cache_ref[...]

    @pl.when(t_conv > 0)
    def _():
        cbuf_ref[0:CONV_HALO, :] = cbuf_ref[tm:tm + CONV_HALO, :]

    mix = (_dot(y_ref[...], wo_ref[0:RWKV_DIM, :])
           + _dot(cact_ref[1 - slot], wo_ref[RWKV_DIM:RWKV_DIM + CONV_CH, :]))
    o_ref[...] = x_ref[...] + mix

    cbuf_ref[CONV_HALO:CONV_HALO + tm, :] = c_ref[...]
    lead = CONV_HALO - (CONV_W - 1)
    n_win = CONV_RB + CONV_HALO
    rep = lambda a: jnp.concatenate([a] * (CONV_RB // SUBLANES), axis=0)
    bias = jnp.broadcast_to(cb_ref[...], (CONV_RB, CONV_CH))
    gain = jnp.broadcast_to(lg_ref[...], (CONV_RB, CONV_CH))
    shift = jnp.broadcast_to(lb_ref[...], (CONV_RB, CONV_CH))
    for rb in range(tm // CONV_RB):
        base = rb * CONV_RB
        parts = []
        for lc in range(CONV_CH // CONV_LANES):
            lanes = slice(lc * CONV_LANES, (lc + 1) * CONV_LANES)
            win = cbuf_ref[base:base + n_win, lanes]
            acc = None
            for sub in range(SUBLANES):
                taps = [j for j in range(CONV_W) if (lead + j) % SUBLANES == sub]
                if not taps:
                    continue
                shifted = win if sub == 0 else pltpu.roll(win, n_win - sub, 0)
                for j in taps:
                    off = lead + j - sub
                    term = rep(cw_ref[j, :, lanes]) * shifted[off:off + CONV_RB, :]
                    acc = term if acc is None else acc + term
            parts.append(acc)
        acc = jnp.concatenate(parts, axis=1) + bias
        mean = jnp.mean(acc, axis=-1, keepdims=True)
        d = acc - mean
        var = jnp.mean(d * d, axis=-1, keepdims=True)
        yn = d * lax.rsqrt(var + LN_EPS) * gain + shift
        cact_ref[slot, base:base + CONV_RB, :] = (yn * jax.nn.sigmoid(yn)).astype(BF16)


def _mixout(y3, c3, cache, wts, x3, tm):
    b, t_total, _ = x3.shape
    tps = t_total // tm
    n_tiles = b * tps
    conv_tile = lambda s: jnp.minimum(s, n_tiles - 1)
    proj_tile = lambda s: jnp.maximum(s - 1, 0)

    def const(shape):
        nd = len(shape)
        return pl.BlockSpec(shape, lambda s, _nd=nd: (0,) * _nd)

    def rows(width, tile_of):
        return pl.BlockSpec((None, tm, width), lambda s: (tile_of(s) // tps, tile_of(s) % tps, 0))

    return pl.pallas_call(
        functools.partial(_mixout_kernel, tm=tm, tiles_per_seq=tps, n_tiles=n_tiles),
        grid=(n_tiles + 1,),
        in_specs=[
            rows(RWKV_DIM, proj_tile),
            rows(CONV_CH, conv_tile),
            pl.BlockSpec((None, CONV_HALO, CONV_CH), lambda s: (conv_tile(s) // tps, 0, 0)),
            const((CONV_HALO, SUBLANES, CONV_CH)), const((1, CONV_CH)), const((1, CONV_CH)), const((1, CONV_CH)),
            const((RWKV_DIM + CONV_CH, D_MODEL)),
            rows(D_MODEL, proj_tile),
        ],
        out_specs=rows(D_MODEL, proj_tile),
        out_shape=jax.ShapeDtypeStruct((b, t_total, D_MODEL), F32),
        scratch_shapes=[
            pltpu.VMEM((tm + CONV_HALO, CONV_CH), F32),
            pltpu.VMEM((2, tm, CONV_CH), BF16),
        ],
        compiler_params=_cparams(("arbitrary",)),
        name="mixout",
    )(y3, c3, cache, wts["conv_w"], wts["conv_b"], wts["conv_ln_g"], wts["conv_ln_b"], wts["w_out"], x3)


def _gelu_tanh(x):
    return 0.5 * x * (1.0 + jnp.tanh(math.sqrt(2.0 / math.pi) * (x + 0.044715 * (x * x * x))))


def _ffn_kernel(x_ref, gn_ref, wu_ref, wz_ref, fw_ref, fb_ref, wd_ref, cache_ref, fn_ref,
                o_ref, tail_ref, h_ref, *, tm, n_seq, seq_rows, tail_off):
    i = pl.program_id(1)
    j = pl.program_id(2)

    @pl.when((i == 0) & (j == 0))
    def _():
        tail_ref[...] = cache_ref[...]

    @pl.when(j == 0)
    def _():
        x = x_ref[...]
        ms = jnp.mean(x * x, axis=-1, keepdims=True)
        h_ref[...] = (x * lax.rsqrt(ms + RMS_EPS) * gn_ref[...]).astype(BF16)
        o_ref[...] = jnp.zeros_like(o_ref)

    h = h_ref[...]
    u = _dot(h, wu_ref[...])
    z = _dot(h, wz_ref[...])
    prev = tail_ref[j]
    row = lax.broadcasted_iota(jnp.int32, (tm, 1), 0)
    u_m1 = pltpu.roll(u, 1, 0)
    u_m2 = pltpu.roll(u, 2, 0)
    for s in range(n_seq):
        pm1 = prev[(s + 1) * SUBLANES - 1:(s + 1) * SUBLANES, :]
        pm2 = prev[(s + 1) * SUBLANES - 2:(s + 1) * SUBLANES - 1, :]
        u_m1 = jnp.where(row == s * seq_rows, pm1, u_m1)
        u_m2 = jnp.where(row == s * seq_rows, pm2, jnp.where(row == s * seq_rows + 1, pm1, u_m2))
    uc = fw_ref[0:1, :] * u_m2 + fw_ref[1:2, :] * u_m1 + fw_ref[2:3, :] * u + fb_ref[...]

    for s in range(n_seq):
        end = s * seq_rows + tail_off
        tail_ref[j, s * SUBLANES:(s + 1) * SUBLANES, :] = u[end - SUBLANES:end, :]

    act = (_gelu_tanh(uc) * z).astype(BF16)
    o_ref[...] += _dot(act, wd_ref[...])

    @pl.when(j == pl.num_programs(2) - 1)
    def _():
        x2 = x_ref[...] + o_ref[...]
        ms = jnp.mean(x2 * x2, axis=-1, keepdims=True)
        o_ref[...] = x2 * lax.rsqrt(ms + RMS_EPS) * fn_ref[...]


def _ffn(x3, cache, wts, tm, t_valid):
    b0, t0, _ = x3.shape
    n_seq = max(tm // t0, 1)
    assert b0 % n_seq == 0 and (tm % t0 == 0 or t0 % tm == 0)
    b, t_total = b0 // n_seq, t0 * n_seq
    x3 = x3.reshape(b, t_total, D_MODEL)
    n_i = t_total // tm
    assert t_valid == t0 or n_i == 1
    tail_off = t_valid if n_i == 1 else tm
    assert tail_off >= SUBLANES
    tail_rows = n_seq * SUBLANES
    cache = jnp.concatenate([jnp.zeros((b0, SUBLANES - (FFN_CONV_W - 1), D_FF), F32), cache.astype(F32)], axis=1)
    cache = cache.reshape(b, tail_rows, N_FF_TILES, FF_TILE).transpose(0, 2, 1, 3)

    def const(shape):
        nd = len(shape)
        return pl.BlockSpec(shape, lambda s, i, j, _nd=nd: (0,) * _nd)

    kern = functools.partial(_ffn_kernel, tm=tm, n_seq=n_seq, seq_rows=t0, tail_off=tail_off)
    y, tail = pl.pallas_call(
        kern,
        grid=(b, n_i, N_FF_TILES),
        in_specs=[
            pl.BlockSpec((None, tm, D_MODEL), lambda s, i, j: (s, i, 0), pipeline_mode=pl.Buffered(1)),
            const((1, D_MODEL)),
            pl.BlockSpec((D_MODEL, FF_TILE), lambda s, i, j: (0, j)),
            pl.BlockSpec((D_MODEL, FF_TILE), lambda s, i, j: (0, j + N_FF_TILES)),
            pl.BlockSpec((SUBLANES, FF_TILE), lambda s, i, j: (0, j)),
            pl.BlockSpec((1, FF_TILE), lambda s, i, j: (0, j)),
            pl.BlockSpec((FF_TILE, D_MODEL), lambda s, i, j: (j, 0)),
            pl.BlockSpec((None, N_FF_TILES, tail_rows, FF_TILE), lambda s, i, j: (s, 0, 0, 0)),
            const((1, D_MODEL)),
        ],
        out_specs=[
            pl.BlockSpec((None, tm, D_MODEL), lambda s, i, j: (s, i, 0)),
            pl.BlockSpec((None, N_FF_TILES, tail_rows, FF_TILE), lambda s, i, j: (s, 0, 0, 0)),
        ],
        out_shape=[
            jax.ShapeDtypeStruct((b, t_total, D_MODEL), F32),
            jax.ShapeDtypeStruct((b, N_FF_TILES, tail_rows, FF_TILE), F32),
        ],
        scratch_shapes=[pltpu.VMEM((tm, D_MODEL), BF16)],
        compiler_params=_cparams(("arbitrary", "arbitrary", "arbitrary")),
        name="ffn",
    )(x3, wts["norm_ffn"], wts["w_up"], wts["w_up"], wts["ffn_conv_w"], wts["ffn_conv_b"],
      wts["w_down"], cache, wts["final_norm"])
    new_cache = tail.transpose(0, 2, 1, 3).reshape(b0, SUBLANES, D_FF)[:, SUBLANES - (FFN_CONV_W - 1):]
    return y.reshape(b0, t0, D_MODEL), new_cache


def _pad_cols_rwkv(a):
    d3 = 3 * RWKV_DIM
    lead = a.shape[:-1]
    zeros = lambda w: jnp.zeros(lead + (w,), a.dtype)
    return jnp.concatenate([
        a[..., :d3 + LA_W], zeros(GL_OFF - LA_OFF - LA_W),
        a[..., d3 + LA_W:], zeros(GL_W - GATE_LORA)], axis=-1)


def _unpad_cols_rwkv(a):
    d3 = 3 * RWKV_DIM
    return jnp.concatenate([a[..., :d3 + LA_W], a[..., GL_OFF:GL_OFF + GATE_LORA]], axis=-1)


def _prep_weights(norm_mix, w_in, tshift_mu, w0, w_decay_up, a0, w_aaa_up, w_gate_up, k_k, k_a, r_k,
                  lnx_g, lnx_b, conv_w, conv_b, conv_ln_g, conv_ln_b, w_out, norm_ffn, w_ffn_up,
                  ffn_conv_w, ffn_conv_b, w_ffn_down, final_norm):
    row = lambda a: a.reshape(1, -1).astype(F32)
    w_in_b = w_in.astype(BF16)
    zrows = lambda n: jnp.zeros((n, RWKV_DIM), F32)

    return {
        "norm_mix": row(norm_mix),
        "w_rkv": w_in_b[:, :3 * RWKV_DIM],
        "w_lora": _pad_cols_rwkv(w_in_b[:, :RWKV_COLS])[:, 3 * RWKV_DIM:],
        "w_val": w_in_b[:, RWKV_COLS:RWKV_COLS + CONV_CH],
        "w_gate": w_in_b[:, RWKV_COLS + CONV_CH:],
        "mu": row(_pad_cols_rwkv(tshift_mu)),
        "w0": row(w0),
        "wdec": jnp.concatenate([w_decay_up, zrows(AAA_LORA)], axis=0).astype(BF16),
        "a0": row(a0),
        "waaa": jnp.concatenate([zrows(DECAY_LORA), w_aaa_up], axis=0).astype(BF16),
        "wgate": jnp.concatenate([w_gate_up, zrows(GL_W - GATE_LORA)], axis=0).astype(BF16),
        "k_k": row(k_k), "k_a": row(k_a), "r_k": row(r_k), "lnx_g": row(lnx_g), "lnx_b": row(lnx_b),
        "conv_w": jnp.broadcast_to(
            jnp.concatenate([conv_w, jnp.zeros((CONV_HALO - CONV_W, CONV_CH), F32)], axis=0).astype(F32)[:, None, :],
            (CONV_HALO, SUBLANES, CONV_CH)),
        "conv_b": row(conv_b), "conv_ln_g": row(conv_ln_g), "conv_ln_b": row(conv_ln_b),
        "w_out": w_out.astype(BF16),
        "norm_ffn": row(norm_ffn),
        "w_up": w_ffn_up.astype(BF16),
        "ffn_conv_w": jnp.concatenate([ffn_conv_w, jnp.zeros((SUBLANES - FFN_CONV_W, D_FF), F32)], axis=0).astype(F32),
        "ffn_conv_b": row(ffn_conv_b),
        "w_down": w_ffn_down.astype(BF16),
        "final_norm": row(final_norm),
    }


def _layer(x3, t_valid, shift, wkv_state, conv_cache, ffn_cache, wts, tiles):
    b, t_total, _ = x3.shape
    tm_in, tt, tm_mix, tm_ffn = tiles
    p2, c2 = _inproj(x3.reshape(b * t_total, D_MODEL), wts["norm_mix"], wts, tm_in)
    p3 = p2.reshape(b, t_total, P_COLS)
    c3 = c2.reshape(b, t_total, CONV_CH)

    y3, new_wkv = _wkv(p3, _pad_cols_rwkv(shift.astype(F32)), wkv_state.astype(F32), wts, tt, t_valid)

    cache_pad = jnp.concatenate(
        [jnp.zeros((b, CONV_HALO - (CONV_W - 1), CONV_CH), F32), conv_cache.astype(F32)], axis=1)
    x1 = _mixout(y3, c3, cache_pad, wts, x3, tm_mix)

    y, new_ffn = _ffn(x1, ffn_cache, wts, tm_ffn, t_valid)

    new_shift = _unpad_cols_rwkv(p3[:, t_valid - 1:t_valid, :])
    hist = CONV_W - 1
    if t_valid >= hist:
        new_conv = c3[:, t_valid - hist:t_valid]
    else:
        new_conv = jnp.concatenate([conv_cache.astype(F32)[:, t_valid:], c3[:, :t_valid]], axis=1)
    return y, new_shift, new_wkv, new_conv, new_ffn


def _prompt_tiles(b, t):
    tiles = (min(1024, b * t), min(256, t), min(512, t), min(1024, t))
    assert (b * t) % tiles[0] == 0 and all(t % x == 0 for x in tiles[1:]) and tiles[1] % CHUNK == 0
    return tiles


def kernel(x_prompt, x_sample, state_shift, state_wkv, cache_conv, cache_ffn_conv, meta, norm_mix, w_in,
           tshift_mu, w0, w_decay_up, a0, w_aaa_up, w_gate_up, k_k, k_a, r_k, lnx_g, lnx_b, conv_w, conv_b,
           conv_ln_g, conv_ln_b, w_out, norm_ffn, w_ffn_up, ffn_conv_w, ffn_conv_b, w_ffn_down, final_norm):
    depth = w_in.shape[0]
    bp, t_prompt, _ = x_prompt.shape
    bs, t_s, _ = x_sample.shape
    assert t_s == N_META, "the short streams (meta prefix, sample) share one padded pass"
    dt = x_prompt.dtype

    t_pad = CHUNK
    xs = jnp.concatenate([meta.astype(dt)[None], x_sample], axis=0)
    xs = jnp.pad(xs, ((0, 0), (0, t_pad - t_s), (0, 0)))
    xp = x_prompt
    nb = bs + 1

    outs_p, outs_s = [], []
    for l in range(depth):
        wts = _prep_weights(norm_mix[l], w_in[l], tshift_mu[l], w0[l], w_decay_up[l], a0[l], w_aaa_up[l],
                            w_gate_up[l], k_k[l], k_a[l], r_k[l], lnx_g[l], lnx_b[l], conv_w[l], conv_b[l],
                            conv_ln_g[l], conv_ln_b[l], w_out[l], norm_ffn[l], w_ffn_up[l], ffn_conv_w[l],
                            ffn_conv_b[l], w_ffn_down[l], final_norm)
        z = lambda *s: jnp.zeros(s, F32)
        st_s = (
            jnp.concatenate([z(1, 1, RWKV_COLS), state_shift[l]], axis=0),
            jnp.concatenate([z(1, N_HEADS, HEAD_DIM, HEAD_DIM), state_wkv[l]], axis=0),
            jnp.concatenate([z(1, CONV_W - 1, CONV_CH), cache_conv[l]], axis=0),
            jnp.concatenate([z(1, FFN_CONV_W - 1, D_FF), cache_ffn_conv[l]], axis=0),
        )
        ys, sh_s, wk_s, cv_s, ff_s = _layer(xs, t_s, *st_s, wts, (nb * t_pad, t_pad, t_pad, nb * t_pad))
        bc = lambda a: jnp.broadcast_to(a[0:1], (bp,) + a.shape[1:])
        yp, sh_p, wk_p, cv_p, ff_p = _layer(xp, t_prompt, bc(sh_s), bc(wk_s), bc(cv_s), bc(ff_s), wts,
                                            _prompt_tiles(bp, t_prompt))
        outs_s.append((sh_s[1:], wk_s[1:], cv_s[1:], ff_s[1:]))
        outs_p.append((sh_p, wk_p, cv_p, ff_p))
        assert depth == 1
        xs, xp = ys, yp

    stack = lambda outs, i: jnp.stack([o[i] for o in outs]).astype(dt)
    y_prompt = xp
    y_sample = xs[1:, :t_s]
    return (y_prompt, y_sample,
            stack(outs_p, 0), stack(outs_p, 1), stack(outs_p, 2), stack(outs_p, 3),
            stack(outs_s, 0), stack(outs_s, 1), stack(outs_s, 2), stack(outs_s, 3))
```

```python
import functools
import math

import jax
import jax.numpy as jnp
from jax import lax
from jax.experimental import pallas as pl
from jax.experimental.pallas import tpu as pltpu

F32 = jnp.float32
BF16 = jnp.bfloat16

D_MODEL = 2048
RWKV_DIM = 1024
HEAD_DIM = 64
N_HEADS = RWKV_DIM // HEAD_DIM
DECAY_LORA = 64
AAA_LORA = 64
GATE_LORA = 160
RWKV_COLS = 3 * RWKV_DIM + DECAY_LORA + AAA_LORA + GATE_LORA
CONV_CH = 1024
CONV_W = 31
D_FF = 5632
FFN_CONV_W = 3
N_META = 16
RMS_EPS = 1e-6
LN_EPS = 1e-5
GN_EPS = 1e-5 * HEAD_DIM

MXU_W = 256
SUBLANES = 8
VMEM_LIMIT_BYTES = 56 * 1024 * 1024

GROUP_W = MXU_W
GROUP_HEADS = GROUP_W // HEAD_DIM
N_GROUPS = RWKV_DIM // GROUP_W
CHUNK = MXU_W // GROUP_HEADS
N_DOUBLINGS = int(math.log2(CHUNK)) - 1

COL_TILE = 512
LORA_W = COL_TILE
LA_OFF = 3 * RWKV_DIM
LA_W = DECAY_LORA + AAA_LORA
GL_OFF = LA_OFF + MXU_W
GL_W = MXU_W
P_COLS = 3 * RWKV_DIM + LORA_W
N_RKV_TILES = 3 * RWKV_DIM // COL_TILE
N_P_TILES = P_COLS // COL_TILE
GLU_W = COL_TILE // 2
N_GLU_TILES = CONV_CH // GLU_W
CONV_HALO = 32
CONV_RB = 32
CONV_LANES = 256
FF_TILE = 512
N_FF_TILES = D_FF // FF_TILE


def _cparams(sem):
    return pltpu.CompilerParams(dimension_semantics=sem, vmem_limit_bytes=VMEM_LIMIT_BYTES)


def _dot(a, b):
    return jnp.dot(a, b, preferred_element_type=F32)


def _dot_nt(a, b):
    return lax.dot_general(a, b, (((1,), (1,)), ((), ())), preferred_element_type=F32)


def _dot_tn(a, b):
    return lax.dot_general(a, b, (((0,), (0,)), ((), ())), preferred_element_type=F32)


def _inproj_kernel(x_ref, g_ref, wrkv_ref, wlora_ref, wval_ref, wgate_ref, p_ref, c_ref, h_ref):
    j = pl.program_id(1)

    @pl.when(j == 0)
    def _():
        x = x_ref[...]
        ms = jnp.mean(x * x, axis=-1, keepdims=True)
        h_ref[...] = (x * lax.rsqrt(ms + RMS_EPS) * g_ref[...]).astype(BF16)

    @pl.when(j < N_RKV_TILES)
    def _():
        p_ref[...] = _dot(h_ref[...], wrkv_ref[...])

    @pl.when(j == N_RKV_TILES)
    def _():
        p_ref[...] = _dot(h_ref[...], wlora_ref[...])

    @pl.when(j >= N_P_TILES)
    def _():
        h = h_ref[...]
        c_ref[...] = _dot(h, wval_ref[...]) * jax.nn.sigmoid(_dot(h, wgate_ref[...]))


def _inproj(x2d, g, wts, tm):
    rows = x2d.shape[0]
    n_col = N_P_TILES + N_GLU_TILES
    glu_tile = lambda i, j: (0, jnp.maximum(j - N_P_TILES, 0))
    return pl.pallas_call(
        _inproj_kernel,
        grid=(rows // tm, n_col),
        in_specs=[
            pl.BlockSpec((tm, D_MODEL), lambda i, j: (i, 0)),
            pl.BlockSpec((1, D_MODEL), lambda i, j: (0, 0)),
            pl.BlockSpec((D_MODEL, COL_TILE), lambda i, j: (0, jnp.minimum(j, N_RKV_TILES - 1))),
            pl.BlockSpec((D_MODEL, LORA_W), lambda i, j: (0, 0)),
            pl.BlockSpec((D_MODEL, GLU_W), glu_tile),
            pl.BlockSpec((D_MODEL, GLU_W), glu_tile),
        ],
        out_specs=[
            pl.BlockSpec((tm, COL_TILE), lambda i, j: (i, jnp.minimum(j, N_P_TILES - 1))),
            pl.BlockSpec((tm, GLU_W), lambda i, j: (i, jnp.maximum(j - N_P_TILES, 0))),
        ],
        out_shape=[
            jax.ShapeDtypeStruct((rows, P_COLS), F32),
            jax.ShapeDtypeStruct((rows, CONV_CH), F32),
        ],
        scratch_shapes=[pltpu.VMEM((tm, D_MODEL), BF16)],
        compiler_params=_cparams(("arbitrary", "arbitrary")),
        name="inproj",
    )(x2d, g, wts["w_rkv"], wts["w_lora"], wts["w_val"], wts["w_gate"])


def _wkv_kernel(p_ref, shift0_ref, s0_ref, mu_ref, w0_ref, wdec_ref, a0_ref, waaa_ref, wgate_ref,
                kk_ref, ka_ref, rk_ref, lng_ref, lnb_ref, tri_ref, hmask_ref,
                y_ref, sout_ref,
                s_ref, prev_ref, rt_ref, at_ref, kt_ref, bt_ref, kh_ref, bh_ref, vv_ref, wa_ref, rp_ref, el_ref,
                yacc_ref, bonus_ref, gate_ref, u0_ref, *, tt, t_valid, t_total):
    t = pl.program_id(1)

    @pl.when(t == 0)
    def _():
        prev_ref[...] = shift0_ref[...]
        zero = jnp.zeros((HEAD_DIM, HEAD_DIM), F32)
        for g in range(N_GROUPS):
            band = [jnp.concatenate([s0_ref[g * GROUP_HEADS + h] if hh == h else zero for hh in range(GROUP_HEADS)],
                                    axis=1) for h in range(GROUP_HEADS)]
            s_ref[g] = jnp.concatenate(band, axis=0)

    hmask_bf = hmask_ref[...]
    head_shift = HEAD_DIM.bit_length() - 1
    hmask = (jnp.right_shift(lax.broadcasted_iota(jnp.int32, (GROUP_W, GROUP_W), 0), head_shift)
             == jnp.right_shift(lax.broadcasted_iota(jnp.int32, (GROUP_W, GROUP_W), 1), head_shift))

    def headsum(x):
        parts = [_dot(x[:, g * GROUP_W:(g + 1) * GROUP_W].astype(BF16), hmask_bf) for g in range(N_GROUPS)]
        return jnp.concatenate(parts, axis=1)

    p = p_ref[...]
    row = lax.broadcasted_iota(jnp.int32, (tt, 1), 0)
    p_prev = jnp.where(row == 0, prev_ref[...], pltpu.roll(p, 1, 0))
    prev_ref[...] = p[tt - 1:tt, :]
    ps = p + (p_prev - p) * mu_ref[...]
    r = ps[:, 0:RWKV_DIM]
    k = ps[:, RWKV_DIM:2 * RWKV_DIM]
    v = ps[:, 2 * RWKV_DIM:3 * RWKV_DIM]
    la = ps[:, LA_OFF:LA_OFF + LA_W]
    gl = ps[:, GL_OFF:GL_OFF + GL_W]

    z = w0_ref[...] + _dot(jnp.tanh(la).astype(BF16), wdec_ref[...])
    lw = -math.exp(-0.5) * jax.nn.sigmoid(z)
    a_lr = jax.nn.sigmoid(a0_ref[...] + _dot(la.astype(BF16), waaa_ref[...]))
    gate_ref[...] = _dot(jax.nn.sigmoid(gl).astype(BF16), wgate_ref[...])

    kk = k * kk_ref[...]
    kkn = kk * lax.rsqrt(jnp.maximum(headsum(kk * kk), 1e-24))
    k2 = k * (1.0 + (a_lr - 1.0) * ka_ref[...])
    bvec = kkn * a_lr
    avec = -kkn
    bonus_ref[...] = headsum(r * k2 * rk_ref[...]) * v

    if t_valid < t_total:
        valid = (t * tt + row) < t_valid
        lw = jnp.where(valid, lw, 0.0)
        k2 = jnp.where(valid, k2, 0.0)
        bvec = jnp.where(valid, bvec, 0.0)
        avec = jnp.where(valid, avec, 0.0)

    lw_hi = lw.astype(BF16)
    lw_lo = (lw - lw_hi.astype(F32)).astype(BF16)
    tri = tri_ref[...]
    cs = _dot(tri, lw_hi) + _dot(tri, lw_lo)
    ec = jnp.exp(cs)
    emc = jnp.exp(-cs)
    kt = k2 * emc
    bt = bvec * emc
    rt_ref[...] = (r * ec).astype(BF16)
    at_ref[...] = (avec * jnp.exp(cs - lw)).astype(BF16)
    kt_ref[...] = kt.astype(BF16)
    bt_ref[...] = bt.astype(BF16)
    vv_ref[...] = v.astype(BF16)
    for c in range(tt // CHUNK):
        rows = slice(c * CHUNK, (c + 1) * CHUNK)
        el = ec[(c + 1) * CHUNK - 1:(c + 1) * CHUNK, :]
        el_ref[c:c + 1, :] = el
        kh_ref[rows, :] = (kt[rows, :] * el).astype(BF16)
        bh_ref[rows, :] = (bt[rows, :] * el).astype(BF16)

    t_idx = lax.broadcasted_iota(jnp.int32, (CHUNK, GROUP_W), 0)
    s_idx = jnp.bitwise_and(lax.broadcasted_iota(jnp.int32, (CHUNK, GROUP_W), 1), CHUNK - 1)
    strict = t_idx > s_idx
    incl = t_idx >= s_idx
    eye = (t_idx == s_idx).astype(F32)

    def bdiag(zb):
        return jnp.concatenate([zb] * GROUP_HEADS, axis=0) * hmask_bf

    groups = range(N_GROUPS)
    cols = [slice(g * GROUP_W, (g + 1) * GROUP_W) for g in groups]
    rowcat = lambda a, b: jnp.concatenate([a, b], axis=0)

    for c in range(tt // CHUNK):
        rows = slice(c * CHUNK, (c + 1) * CHUNK)
        rt = [rt_ref[rows, cols[g]] for g in groups]
        at = [at_ref[rows, cols[g]] for g in groups]
        bd_v = [bdiag(vv_ref[rows, cols[g]]) for g in groups]
        lhs2 = [rowcat(at[g], rt[g]) for g in groups]
        gb = [_dot_nt(lhs2[g], bdiag(bt_ref[rows, cols[g]])) for g in groups]
        gk = [_dot_nt(lhs2[g], bdiag(kt_ref[rows, cols[g]])) for g in groups]
        a_ab = [jnp.where(strict, gb[g][:CHUNK], 0.0) for g in groups]
        a_rb = [jnp.where(incl, gb[g][CHUNK:], 0.0).astype(BF16) for g in groups]
        a_kk = [rowcat(jnp.where(strict, gk[g][:CHUNK], 0.0), jnp.where(incl, gk[g][CHUNK:], 0.0)).astype(BF16)
                for g in groups]
        pw_b = [a_ab[g].astype(BF16) for g in groups]
        tinv = [eye + a_ab[g] for g in groups]
        pw_b = [_dot(pw_b[g], bdiag(pw_b[g])).astype(BF16) for g in groups]
        for _ in range(N_DOUBLINGS - 1):
            res = [_dot(rowcat(pw_b[g], tinv[g].astype(BF16)), bdiag(pw_b[g])) for g in groups]
            pw_b = [res[g][:CHUNK].astype(BF16) for g in groups]
            tinv = [tinv[g] + res[g][CHUNK:] for g in groups]
        t_b = [(tinv[g] + _dot(tinv[g].astype(BF16), bdiag(pw_b[g]))).astype(BF16) for g in groups]
        av2 = [_dot(a_kk[g], bd_v[g]) for g in groups]
        w_a = [_dot(t_b[g], bdiag(at[g])).astype(BF16) for g in groups]
        u0 = [_dot(t_b[g], bdiag(av2[g][:CHUNK].astype(BF16))) for g in groups]
        for g in groups:
            wa_ref[rows, cols[g]] = w_a[g]
            u0_ref[rows, cols[g]] = u0[g]
            rp_ref[rows, cols[g]] = (rt[g].astype(F32) + _dot(a_rb[g], bdiag(w_a[g]))).astype(BF16)
            yacc_ref[rows, cols[g]] = _dot(a_rb[g], bdiag(u0[g].astype(BF16))) + av2[g][CHUNK:]

    for c in range(tt // CHUNK):
        rows = slice(c * CHUNK, (c + 1) * CHUNK)
        s_old = [s_ref[g] for g in groups]
        res = [_dot_nt(rowcat(wa_ref[rows, cols[g]], rp_ref[rows, cols[g]]), s_old[g].astype(BF16)) for g in groups]
        u_b = [(res[g][:CHUNK] + u0_ref[rows, cols[g]]).astype(BF16) for g in groups]
        upd = [_dot_tn(rowcat(u_b[g], vv_ref[rows, cols[g]]), rowcat(bh_ref[rows, cols[g]], kh_ref[rows, cols[g]]))
               for g in groups]
        for g in groups:
            yacc_ref[rows, cols[g]] = res[g][CHUNK:] + yacc_ref[rows, cols[g]]
            w_l = el_ref[c:c + 1, cols[g]]
            s_ref[g] = s_old[g] * w_l + jnp.where(hmask, upd[g], 0.0)

    y = yacc_ref[...]
    mean = headsum(y) * (1.0 / HEAD_DIM)
    d = y - mean
    var = headsum(d * d) * (1.0 / HEAD_DIM)
    yn = d * lax.rsqrt(var + GN_EPS) * lng_ref[...] + lnb_ref[...]
    y_ref[...] = ((yn + bonus_ref[...]) * gate_ref[...]).astype(BF16)

    @pl.when(t == pl.num_programs(1) - 1)
    def _():
        for g in range(N_GROUPS):
            s_g = s_ref[g]
            for h in range(GROUP_HEADS):
                lo = h * HEAD_DIM
                sout_ref[g * GROUP_HEADS + h] = s_g[lo:lo + HEAD_DIM, lo:lo + HEAD_DIM]


def _wkv(p3, shift0, s0, wts, tt, t_valid):
    b, t_total, _ = p3.shape
    idx = jnp.arange(tt)
    same_chunk = (idx[:, None] // CHUNK) == (idx[None, :] // CHUNK)
    tri = (same_chunk & (idx[None, :] <= idx[:, None])).astype(BF16)
    gidx = jnp.arange(GROUP_W) // HEAD_DIM
    hmask = (gidx[:, None] == gidx[None, :]).astype(BF16)

    def const(shape):
        nd = len(shape)
        return pl.BlockSpec(shape, lambda i, j, _nd=nd: (0,) * _nd)

    row = lambda w: const((1, w))
    kern = functools.partial(_wkv_kernel, tt=tt, t_valid=t_valid, t_total=t_total)
    act = lambda dt: pltpu.VMEM((tt, RWKV_DIM), dt)
    return pl.pallas_call(
        kern,
        grid=(b, t_total // tt),
        in_specs=[
            pl.BlockSpec((None, tt, P_COLS), lambda i, j: (i, j, 0)),
            pl.BlockSpec((None, 1, P_COLS), lambda i, j: (i, 0, 0)),
            pl.BlockSpec((None, N_HEADS, HEAD_DIM, HEAD_DIM), lambda i, j: (i, 0, 0, 0)),
            row(P_COLS), row(RWKV_DIM), const((LA_W, RWKV_DIM)), row(RWKV_DIM), const((LA_W, RWKV_DIM)),
            const((GL_W, RWKV_DIM)), row(RWKV_DIM), row(RWKV_DIM), row(RWKV_DIM), row(RWKV_DIM), row(RWKV_DIM),
            const((tt, tt)), const((GROUP_W, GROUP_W)),
        ],
        out_specs=[
            pl.BlockSpec((None, tt, RWKV_DIM), lambda i, j: (i, j, 0)),
            pl.BlockSpec((None, N_HEADS, HEAD_DIM, HEAD_DIM), lambda i, j: (i, 0, 0, 0)),
        ],
        out_shape=[
            jax.ShapeDtypeStruct((b, t_total, RWKV_DIM), BF16),
            jax.ShapeDtypeStruct((b, N_HEADS, HEAD_DIM, HEAD_DIM), F32),
        ],
        scratch_shapes=[
            pltpu.VMEM((N_GROUPS, GROUP_W, GROUP_W), F32),
            pltpu.VMEM((1, P_COLS), F32),
            act(BF16), act(BF16), act(BF16), act(BF16), act(BF16), act(BF16), act(BF16), act(BF16), act(BF16),
            pltpu.VMEM((max(tt // CHUNK, SUBLANES), RWKV_DIM), F32),
            act(F32), act(F32), act(F32), act(F32),
        ],
        compiler_params=_cparams(("arbitrary", "arbitrary")),
        name="wkv",
    )(p3, shift0, s0, wts["mu"], wts["w0"], wts["wdec"], wts["a0"], wts["waaa"], wts["wgate"],
      wts["k_k"], wts["k_a"], wts["r_k"], wts["lnx_g"], wts["lnx_b"], tri, hmask)


def _mixout_kernel(y_ref, c_ref, cache_ref, cw_ref, cb_ref, lg_ref, lb_ref, wo_ref, x_ref, o_ref,
                   cbuf_ref, *, tm):
    t = pl.program_id(1)

    @pl.when(t == 0)
    def _():
        cbuf_ref[0:CONV_HALO, :] = cache_ref[...]

    @pl.when(t > 0)
    def _():
        cbuf_ref[0:CONV_HALO, :] = cbuf_ref[tm:tm + CONV_HALO, :]

    cbuf_ref[CONV_HALO:CONV_HALO + tm, :] = c_ref[...]
    lead = CONV_HALO - (CONV_W - 1)
    crb = min(CONV_RB, tm)
    n_win = crb + CONV_HALO
    rep = lambda a: jnp.concatenate([a] * (crb // SUBLANES), axis=0)
    bias = jnp.broadcast_to(cb_ref[...], (crb, CONV_CH))
    gain = jnp.broadcast_to(lg_ref[...], (crb, CONV_CH))
    shift = jnp.broadcast_to(lb_ref[...], (crb, CONV_CH))
    acts = []
    for rb in range(tm // crb):
        base = rb * crb
        parts = []
        for lc in range(CONV_CH // CONV_LANES):
            lanes = slice(lc * CONV_LANES, (lc + 1) * CONV_LANES)
            win = cbuf_ref[base:base + n_win, lanes]
            acc = None
            for sub in range(SUBLANES):
                taps = [j for j in range(CONV_W) if (lead + j) % SUBLANES == sub]
                if not taps:
                    continue
                shifted = win if sub == 0 else pltpu.roll(win, n_win - sub, 0)
                for j in taps:
                    off = lead + j - sub
                    term = rep(cw_ref[j, :, lanes]) * shifted[off:off + crb, :]
                    acc = term if acc is None else acc + term
            parts.append(acc)
        acc = jnp.concatenate(parts, axis=1) + bias
        mean = jnp.mean(acc, axis=-1, keepdims=True)
        d = acc - mean
        var = jnp.mean(d * d, axis=-1, keepdims=True)
        yn = d * lax.rsqrt(var + LN_EPS) * gain + shift
        acts.append((yn * jax.nn.sigmoid(yn)).astype(BF16))

    cact = jnp.concatenate(acts, axis=0)
    mix = _dot(y_ref[...], wo_ref[0:RWKV_DIM, :]) + _dot(cact, wo_ref[RWKV_DIM:RWKV_DIM + CONV_CH, :])
    o_ref[...] = x_ref[...] + mix


def _mixout(y3, c3, cache, wts, x3, tm):
    b, t_total, _ = x3.shape

    def const(shape):
        nd = len(shape)
        return pl.BlockSpec(shape, lambda i, j, _nd=nd: (0,) * _nd)

    rows = lambda width: pl.BlockSpec((None, tm, width), lambda i, j: (i, j, 0))
    return pl.pallas_call(
        functools.partial(_mixout_kernel, tm=tm),
        grid=(b, t_total // tm),
        in_specs=[
            rows(RWKV_DIM),
            rows(CONV_CH),
            pl.BlockSpec((None, CONV_HALO, CONV_CH), lambda i, j: (i, 0, 0)),
            const((CONV_HALO, SUBLANES, CONV_CH)), const((1, CONV_CH)), const((1, CONV_CH)), const((1, CONV_CH)),
            const((RWKV_DIM + CONV_CH, D_MODEL)),
            rows(D_MODEL),
        ],
        out_specs=rows(D_MODEL),
        out_shape=jax.ShapeDtypeStruct((b, t_total, D_MODEL), F32),
        scratch_shapes=[pltpu.VMEM((tm + CONV_HALO, CONV_CH), F32)],
        compiler_params=_cparams(("arbitrary", "arbitrary")),
        name="mixout",
    )(y3, c3, cache, wts["conv_w"], wts["conv_b"], wts["conv_ln_g"], wts["conv_ln_b"], wts["w_out"], x3)


def _gelu_tanh(x):
    return 0.5 * x * (1.0 + jnp.tanh(math.sqrt(2.0 / math.pi) * (x + 0.044715 * (x * x * x))))


def _ffn_kernel(x_ref, gn_ref, wu_ref, wz_ref, fw_ref, fb_ref, wd_ref, cache_ref, fn_ref,
                o_ref, tail_ref, h_ref, *, tm, n_seq, seq_rows, tail_off):
    i = pl.program_id(1)
    j = pl.program_id(2)

    @pl.when((i == 0) & (j == 0))
    def _():
        tail_ref[...] = cache_ref[...]

    @pl.when(j == 0)
    def _():
        x = x_ref[...]
        ms = jnp.mean(x * x, axis=-1, keepdims=True)
        h_ref[...] = (x * lax.rsqrt(ms + RMS_EPS) * gn_ref[...]).astype(BF16)
        o_ref[...] = jnp.zeros_like(o_ref)

    h = h_ref[...]
    u = _dot(h, wu_ref[...])
    z = _dot(h, wz_ref[...])
    prev = tail_ref[j]
    row = lax.broadcasted_iota(jnp.int32, (tm, 1), 0)
    u_m1 = pltpu.roll(u, 1, 0)
    u_m2 = pltpu.roll(u, 2, 0)
    for s in range(n_seq):
        pm1 = prev[(s + 1) * SUBLANES - 1:(s + 1) * SUBLANES, :]
        pm2 = prev[(s + 1) * SUBLANES - 2:(s + 1) * SUBLANES - 1, :]
        u_m1 = jnp.where(row == s * seq_rows, pm1, u_m1)
        u_m2 = jnp.where(row == s * seq_rows, pm2, jnp.where(row == s * seq_rows + 1, pm1, u_m2))
    uc = fw_ref[0:1, :] * u_m2 + fw_ref[1:2, :] * u_m1 + fw_ref[2:3, :] * u + fb_ref[...]

    for s in range(n_seq):
        end = s * seq_rows + tail_off
        tail_ref[j, s * SUBLANES:(s + 1) * SUBLANES, :] = u[end - SUBLANES:end, :]

    act = (_gelu_tanh(uc) * z).astype(BF16)
    o_ref[...] += _dot(act, wd_ref[...])

    @pl.when(j == pl.num_programs(2) - 1)
    def _():
        x2 = x_ref[...] + o_ref[...]
        ms = jnp.mean(x2 * x2, axis=-1, keepdims=True)
        o_ref[...] = x2 * lax.rsqrt(ms + RMS_EPS) * fn_ref[...]


def _ffn(x3, cache, wts, tm, t_valid):
    b0, t0, _ = x3.shape
    n_seq = max(tm // t0, 1)
    assert b0 % n_seq == 0 and (tm % t0 == 0 or t0 % tm == 0)
    b, t_total = b0 // n_seq, t0 * n_seq
    x3 = x3.reshape(b, t_total, D_MODEL)
    n_i = t_total // tm
    assert t_valid == t0 or n_i == 1
    tail_off = t_valid if n_i == 1 else tm
    assert tail_off >= SUBLANES
    tail_rows = n_seq * SUBLANES
    cache = jnp.concatenate([jnp.zeros((b0, SUBLANES - (FFN_CONV_W - 1), D_FF), F32), cache.astype(F32)], axis=1)
    cache = cache.reshape(b, tail_rows, N_FF_TILES, FF_TILE).transpose(0, 2, 1, 3)

    def const(shape):
        nd = len(shape)
        return pl.BlockSpec(shape, lambda s, i, j, _nd=nd: (0,) * _nd)

    kern = functools.partial(_ffn_kernel, tm=tm, n_seq=n_seq, seq_rows=t0, tail_off=tail_off)
    y, tail = pl.pallas_call(
        kern,
        grid=(b, n_i, N_FF_TILES),
        in_specs=[
            pl.BlockSpec((None, tm, D_MODEL), lambda s, i, j: (s, i, 0), pipeline_mode=pl.Buffered(1)),
            const((1, D_MODEL)),
            pl.BlockSpec((D_MODEL, FF_TILE), lambda s, i, j: (0, j)),
            pl.BlockSpec((D_MODEL, FF_TILE), lambda s, i, j: (0, j + N_FF_TILES)),
            pl.BlockSpec((SUBLANES, FF_TILE), lambda s, i, j: (0, j)),
            pl.BlockSpec((1, FF_TILE), lambda s, i, j: (0, j)),
            pl.BlockSpec((FF_TILE, D_MODEL), lambda s, i, j: (j, 0)),
            pl.BlockSpec((None, N_FF_TILES, tail_rows, FF_TILE), lambda s, i, j: (s, 0, 0, 0)),
            const((1, D_MODEL)),
        ],
        out_specs=[
            pl.BlockSpec((None, tm, D_MODEL), lambda s, i, j: (s, i, 0)),
            pl.BlockSpec((None, N_FF_TILES, tail_rows, FF_TILE), lambda s, i, j: (s, 0, 0, 0)),
        ],
        out_shape=[
            jax.ShapeDtypeStruct((b, t_total, D_MODEL), F32),
            jax.ShapeDtypeStruct((b, N_FF_TILES, tail_rows, FF_TILE), F32),
        ],
        scratch_shapes=[pltpu.VMEM((tm, D_MODEL), BF16)],
        compiler_params=_cparams(("arbitrary", "arbitrary", "arbitrary")),
        name="ffn",
    )(x3, wts["norm_ffn"], wts["w_up"], wts["w_up"], wts["ffn_conv_w"], wts["ffn_conv_b"],
      wts["w_down"], cache, wts["final_norm"])
    new_cache = tail.transpose(0, 2, 1, 3).reshape(b0, SUBLANES, D_FF)[:, SUBLANES - (FFN_CONV_W - 1):]
    return y.reshape(b0, t0, D_MODEL), new_cache


def _pad_cols_rwkv(a):
    d3 = 3 * RWKV_DIM
    lead = a.shape[:-1]
    zeros = lambda w: jnp.zeros(lead + (w,), a.dtype)
    return jnp.concatenate([
        a[..., :d3 + LA_W], zeros(GL_OFF - LA_OFF - LA_W),
        a[..., d3 + LA_W:], zeros(GL_W - GATE_LORA)], axis=-1)


def _unpad_cols_rwkv(a):
    d3 = 3 * RWKV_DIM
    return jnp.concatenate([a[..., :d3 + LA_W], a[..., GL_OFF:GL_OFF + GATE_LORA]], axis=-1)


def _prep_weights(norm_mix, w_in, tshift_mu, w0, w_decay_up, a0, w_aaa_up, w_gate_up, k_k, k_a, r_k,
                  lnx_g, lnx_b, conv_w, conv_b, conv_ln_g, conv_ln_b, w_out, norm_ffn, w_ffn_up,
                  ffn_conv_w, ffn_conv_b, w_ffn_down, final_norm):
    row = lambda a: a.reshape(1, -1).astype(F32)
    w_in_b = w_in.astype(BF16)
    zrows = lambda n: jnp.zeros((n, RWKV_DIM), F32)

    return {
        "norm_mix": row(norm_mix),
        "w_rkv": w_in_b[:, :3 * RWKV_DIM],
        "w_lora": _pad_cols_rwkv(w_in_b[:, :RWKV_COLS])[:, 3 * RWKV_DIM:],
        "w_val": w_in_b[:, RWKV_COLS:RWKV_COLS + CONV_CH],
        "w_gate": w_in_b[:, RWKV_COLS + CONV_CH:],
        "mu": row(_pad_cols_rwkv(tshift_mu)),
        "w0": row(w0),
        "wdec": jnp.concatenate([w_decay_up, zrows(AAA_LORA)], axis=0).astype(BF16),
        "a0": row(a0),
        "waaa": jnp.concatenate([zrows(DECAY_LORA), w_aaa_up], axis=0).astype(BF16),
        "wgate": jnp.concatenate([w_gate_up, zrows(GL_W - GATE_LORA)], axis=0).astype(BF16),
        "k_k": row(k_k), "k_a": row(k_a), "r_k": row(r_k), "lnx_g": row(lnx_g), "lnx_b": row(lnx_b),
        "conv_w": jnp.broadcast_to(
            jnp.concatenate([conv_w, jnp.zeros((CONV_HALO - CONV_W, CONV_CH), F32)], axis=0).astype(F32)[:, None, :],
            (CONV_HALO, SUBLANES, CONV_CH)),
        "conv_b": row(conv_b), "conv_ln_g": row(conv_ln_g), "conv_ln_b": row(conv_ln_b),
        "w_out": w_out.astype(BF16),
        "norm_ffn": row(norm_ffn),
        "w_up": w_ffn_up.astype(BF16),
        "ffn_conv_w": jnp.concatenate([ffn_conv_w, jnp.zeros((SUBLANES - FFN_CONV_W, D_FF), F32)], axis=0).astype(F32),
        "ffn_conv_b": row(ffn_conv_b),
        "w_down": w_ffn_down.astype(BF16),
        "final_norm": row(final_norm),
    }


def _layer(x3, shift, wkv_state, conv_cache, ffn_cache, wts, tiles):
    b, t_valid, _ = x3.shape
    tm_in, tt, tm_mix, tm_ffn = tiles
    p2, c2 = _inproj(x3.reshape(b * t_valid, D_MODEL), wts["norm_mix"], wts, tm_in)
    p3 = p2.reshape(b, t_valid, P_COLS)
    c3 = c2.reshape(b, t_valid, CONV_CH)

    t_scan = -(-t_valid // CHUNK) * CHUNK
    p_scan = jnp.pad(p3, ((0, 0), (0, t_scan - t_valid), (0, 0)))
    y3, new_wkv = _wkv(p_scan, _pad_cols_rwkv(shift.astype(F32)), wkv_state.astype(F32), wts, tt, t_valid)
    y3 = y3[:, :t_valid]

    cache_pad = jnp.concatenate(
        [jnp.zeros((b, CONV_HALO - (CONV_W - 1), CONV_CH), F32), conv_cache.astype(F32)], axis=1)
    x1 = _mixout(y3, c3, cache_pad, wts, x3, tm_mix)

    y, new_ffn = _ffn(x1, ffn_cache, wts, tm_ffn, t_valid)

    new_shift = _unpad_cols_rwkv(p3[:, t_valid - 1:t_valid, :])
    hist = CONV_W - 1
    if t_valid >= hist:
        new_conv = c3[:, t_valid - hist:t_valid]
    else:
        new_conv = jnp.concatenate([conv_cache.astype(F32)[:, t_valid:], c3[:, :t_valid]], axis=1)
    return y, new_shift, new_wkv, new_conv, new_ffn


def _prompt_tiles(b, t):
    tiles = (min(1024, b * t), min(256, t), min(512, t), min(1024, t))
    assert (b * t) % tiles[0] == 0 and all(t % x == 0 for x in tiles[1:]) and tiles[1] % CHUNK == 0
    return tiles


def kernel(x_prompt, x_sample, state_shift, state_wkv, cache_conv, cache_ffn_conv, meta, norm_mix, w_in,
           tshift_mu, w0, w_decay_up, a0, w_aaa_up, w_gate_up, k_k, k_a, r_k, lnx_g, lnx_b, conv_w, conv_b,
           conv_ln_g, conv_ln_b, w_out, norm_ffn, w_ffn_up, ffn_conv_w, ffn_conv_b, w_ffn_down, final_norm):
    depth = w_in.shape[0]
    bp, t_prompt, _ = x_prompt.shape
    bs, t_s, _ = x_sample.shape
    assert t_s == N_META, "the short streams (meta prefix, sample) share one padded pass"
    dt = x_prompt.dtype

    xs = jnp.concatenate([meta.astype(dt)[None], x_sample], axis=0)
    xp = x_prompt
    nb = bs + 1

    outs_p, outs_s = [], []
    for l in range(depth):
        wts = _prep_weights(norm_mix[l], w_in[l], tshift_mu[l], w0[l], w_decay_up[l], a0[l], w_aaa_up[l],
                            w_gate_up[l], k_k[l], k_a[l], r_k[l], lnx_g[l], lnx_b[l], conv_w[l], conv_b[l],
                            conv_ln_g[l], conv_ln_b[l], w_out[l], norm_ffn[l], w_ffn_up[l], ffn_conv_w[l],
                            ffn_conv_b[l], w_ffn_down[l], final_norm)
        z = lambda *s: jnp.zeros(s, F32)
        st_s = (
            jnp.concatenate([z(1, 1, RWKV_COLS), state_shift[l]], axis=0),
            jnp.concatenate([z(1, N_HEADS, HEAD_DIM, HEAD_DIM), state_wkv[l]], axis=0),
            jnp.concatenate([z(1, CONV_W - 1, CONV_CH), cache_conv[l]], axis=0),
            jnp.concatenate([z(1, FFN_CONV_W - 1, D_FF), cache_ffn_conv[l]], axis=0),
        )
        ys, sh_s, wk_s, cv_s, ff_s = _layer(xs, *st_s, wts, (nb * t_s, CHUNK, t_s, nb * t_s))
        bc = lambda a: jnp.broadcast_to(a[0:1], (bp,) + a.shape[1:])
        yp, sh_p, wk_p, cv_p, ff_p = _layer(xp, bc(sh_s), bc(wk_s), bc(cv_s), bc(ff_s), wts,
                                            _prompt_tiles(bp, t_prompt))
        outs_s.append((sh_s[1:], wk_s[1:], cv_s[1:], ff_s[1:]))
        outs_p.append((sh_p, wk_p, cv_p, ff_p))
        assert depth == 1
        xs, xp = ys, yp

    stack = lambda outs, i: jnp.stack([o[i] for o in outs]).astype(dt)
    y_prompt = xp
    y_sample = xs[1:]
    return (y_prompt, y_sample,
            stack(outs_p, 0), stack(outs_p, 1), stack(outs_p, 2), stack(outs_p, 3),
            stack(outs_s, 0), stack(outs_s, 1), stack(outs_s, 2), stack(outs_s, 3))
```

```python
import functools
import math

import jax
import jax.numpy as jnp
from jax import lax
from jax.experimental import pallas as pl
from jax.experimental.pallas import tpu as pltpu

F32 = jnp.float32
BF16 = jnp.bfloat16

D_MODEL = 2048
RWKV_DIM = 1024
HEAD_DIM = 64
N_HEADS = RWKV_DIM // HEAD_DIM
DECAY_LORA = 64
AAA_LORA = 64
GATE_LORA = 160
RWKV_COLS = 3 * RWKV_DIM + DECAY_LORA + AAA_LORA + GATE_LORA
CONV_CH = 1024
CONV_W = 31
D_FF = 5632
FFN_CONV_W = 3
N_META = 16
RMS_EPS = 1e-6
LN_EPS = 1e-5
GN_EPS = 1e-5 * HEAD_DIM

MXU_W = 256
SUBLANES = 8
VMEM_LIMIT_BYTES = 56 * 1024 * 1024

GROUP_W = MXU_W
GROUP_HEADS = GROUP_W // HEAD_DIM
N_GROUPS = RWKV_DIM // GROUP_W
CHUNK = MXU_W // GROUP_HEADS
N_DOUBLINGS = int(math.log2(CHUNK)) - 1

COL_TILE = 512
LORA_W = COL_TILE
LA_OFF = 3 * RWKV_DIM
LA_W = DECAY_LORA + AAA_LORA
GL_OFF = LA_OFF + MXU_W
GL_W = MXU_W
P_COLS = 3 * RWKV_DIM + LORA_W
N_RKV_TILES = 3 * RWKV_DIM // COL_TILE
N_P_TILES = P_COLS // COL_TILE
GLU_W = COL_TILE // 2
N_GLU_TILES = CONV_CH // GLU_W
CONV_HALO = 32
CONV_RB = 32
CONV_LANES = 256
FF_TILE = 512
N_FF_TILES = D_FF // FF_TILE


def _cparams(sem):
    return pltpu.CompilerParams(dimension_semantics=sem, vmem_limit_bytes=VMEM_LIMIT_BYTES)


def _dot(a, b):
    return jnp.dot(a, b, preferred_element_type=F32)


def _dot_nt(a, b):
    return lax.dot_general(a, b, (((1,), (1,)), ((), ())), preferred_element_type=F32)


def _dot_tn(a, b):
    return lax.dot_general(a, b, (((0,), (0,)), ((), ())), preferred_element_type=F32)


def _inproj_kernel(x_ref, g_ref, wrkv_ref, wlora_ref, wval_ref, wgate_ref, p_ref, c_ref, h_ref):
    j = pl.program_id(1)

    @pl.when(j == 0)
    def _():
        x = x_ref[...]
        ms = jnp.mean(x * x, axis=-1, keepdims=True)
        h_ref[...] = (x * lax.rsqrt(ms + RMS_EPS) * g_ref[...]).astype(BF16)

    @pl.when(j < N_RKV_TILES)
    def _():
        p_ref[...] = _dot(h_ref[...], wrkv_ref[...]).astype(p_ref.dtype)

    @pl.when(j == N_RKV_TILES)
    def _():
        p_ref[...] = _dot(h_ref[...], wlora_ref[...]).astype(p_ref.dtype)

    @pl.when(j >= N_P_TILES)
    def _():
        h = h_ref[...]
        c_ref[...] = _dot(h, wval_ref[...]) * jax.nn.sigmoid(_dot(h, wgate_ref[...]))


def _inproj(x2d, g, wts, tm):
    rows = x2d.shape[0]
    n_col = N_P_TILES + N_GLU_TILES
    glu_tile = lambda i, j: (0, jnp.maximum(j - N_P_TILES, 0))
    return pl.pallas_call(
        _inproj_kernel,
        grid=(rows // tm, n_col),
        in_specs=[
            pl.BlockSpec((tm, D_MODEL), lambda i, j: (i, 0)),
            pl.BlockSpec((1, D_MODEL), lambda i, j: (0, 0)),
            pl.BlockSpec((D_MODEL, COL_TILE), lambda i, j: (0, jnp.minimum(j, N_RKV_TILES - 1))),
            pl.BlockSpec((D_MODEL, LORA_W), lambda i, j: (0, 0)),
            pl.BlockSpec((D_MODEL, GLU_W), glu_tile),
            pl.BlockSpec((D_MODEL, GLU_W), glu_tile),
        ],
        out_specs=[
            pl.BlockSpec((tm, COL_TILE), lambda i, j: (i, jnp.minimum(j, N_P_TILES - 1))),
            pl.BlockSpec((tm, GLU_W), lambda i, j: (i, jnp.maximum(j - N_P_TILES, 0))),
        ],
        out_shape=[
            jax.ShapeDtypeStruct((rows, P_COLS), BF16),
            jax.ShapeDtypeStruct((rows, CONV_CH), F32),
        ],
        scratch_shapes=[pltpu.VMEM((tm, D_MODEL), BF16)],
        compiler_params=_cparams(("arbitrary", "arbitrary")),
        name="inproj",
    )(x2d, g, wts["w_rkv"], wts["w_lora"], wts["w_val"], wts["w_gate"])


def _wkv_kernel(p_ref, shift0_ref, s0_ref, mu_ref, w0_ref, wdec_ref, a0_ref, waaa_ref, wgate_ref,
                kk_ref, ka_ref, rk_ref, lng_ref, lnb_ref, tri_ref, hmask_ref,
                y_ref, sout_ref,
                s_ref, prev_ref, rt_ref, at_ref, kt_ref, bt_ref, kh_ref, bh_ref, vv_ref, wa_ref, rp_ref, el_ref,
                yacc_ref, bonus_ref, gate_ref, u0_ref, *, tt, t_valid, t_total):
    t = pl.program_id(1)

    @pl.when(t == 0)
    def _():
        prev_ref[...] = shift0_ref[...]
        zero = jnp.zeros((HEAD_DIM, HEAD_DIM), F32)
        for g in range(N_GROUPS):
            band = [jnp.concatenate([s0_ref[g * GROUP_HEADS + h] if hh == h else zero for hh in range(GROUP_HEADS)],
                                    axis=1) for h in range(GROUP_HEADS)]
            s_ref[g] = jnp.concatenate(band, axis=0)

    hmask_bf = hmask_ref[...]
    head_shift = HEAD_DIM.bit_length() - 1
    hmask = (jnp.right_shift(lax.broadcasted_iota(jnp.int32, (GROUP_W, GROUP_W), 0), head_shift)
             == jnp.right_shift(lax.broadcasted_iota(jnp.int32, (GROUP_W, GROUP_W), 1), head_shift))

    def headsum(x):
        parts = [_dot(x[:, g * GROUP_W:(g + 1) * GROUP_W].astype(BF16), hmask_bf) for g in range(N_GROUPS)]
        return jnp.concatenate(parts, axis=1)

    p = p_ref[...].astype(F32)
    row = lax.broadcasted_iota(jnp.int32, (tt, 1), 0)
    p_prev = jnp.where(row == 0, prev_ref[...], pltpu.roll(p, 1, 0))
    prev_ref[...] = p[tt - 1:tt, :]
    ps = p + (p_prev - p) * mu_ref[...]
    r = ps[:, 0:RWKV_DIM]
    k = ps[:, RWKV_DIM:2 * RWKV_DIM]
    v = ps[:, 2 * RWKV_DIM:3 * RWKV_DIM]
    la = ps[:, LA_OFF:LA_OFF + LA_W]
    gl = ps[:, GL_OFF:GL_OFF + GL_W]

    z = w0_ref[...] + _dot(jnp.tanh(la).astype(BF16), wdec_ref[...])
    lw = -math.exp(-0.5) * jax.nn.sigmoid(z)
    a_lr = jax.nn.sigmoid(a0_ref[...] + _dot(la.astype(BF16), waaa_ref[...]))
    gate_ref[...] = _dot(jax.nn.sigmoid(gl).astype(BF16), wgate_ref[...])

    kk = k * kk_ref[...]
    kkn = kk * lax.rsqrt(jnp.maximum(headsum(kk * kk), 1e-24))
    k2 = k * (1.0 + (a_lr - 1.0) * ka_ref[...])
    bvec = kkn * a_lr
    avec = -kkn
    bonus_ref[...] = headsum(r * k2 * rk_ref[...]) * v

    if t_valid < t_total:
        valid = (t * tt + row) < t_valid
        lw = jnp.where(valid, lw, 0.0)
        k2 = jnp.where(valid, k2, 0.0)
        bvec = jnp.where(valid, bvec, 0.0)
        avec = jnp.where(valid, avec, 0.0)

    lw_hi = lw.astype(BF16)
    lw_lo = (lw - lw_hi.astype(F32)).astype(BF16)
    tri = tri_ref[...]
    cs = _dot(tri, lw_hi) + _dot(tri, lw_lo)
    ec = jnp.exp(cs)
    emc = jnp.exp(-cs)
    kt = k2 * emc
    bt = bvec * emc
    rt_ref[...] = (r * ec).astype(BF16)
    at_ref[...] = (avec * jnp.exp(cs - lw)).astype(BF16)
    kt_ref[...] = kt.astype(BF16)
    bt_ref[...] = bt.astype(BF16)
    vv_ref[...] = v.astype(BF16)
    for c in range(tt // CHUNK):
        rows = slice(c * CHUNK, (c + 1) * CHUNK)
        el = ec[(c + 1) * CHUNK - 1:(c + 1) * CHUNK, :]
        el_ref[c:c + 1, :] = el
        kh_ref[rows, :] = (kt[rows, :] * el).astype(BF16)
        bh_ref[rows, :] = (bt[rows, :] * el).astype(BF16)

    t_idx = lax.broadcasted_iota(jnp.int32, (CHUNK, GROUP_W), 0)
    s_idx = jnp.bitwise_and(lax.broadcasted_iota(jnp.int32, (CHUNK, GROUP_W), 1), CHUNK - 1)
    strict = t_idx > s_idx
    incl = t_idx >= s_idx
    eye = (t_idx == s_idx).astype(F32)

    def bdiag(zb):
        return jnp.concatenate([zb] * GROUP_HEADS, axis=0) * hmask_bf

    groups = range(N_GROUPS)
    cols = [slice(g * GROUP_W, (g + 1) * GROUP_W) for g in groups]
    rowcat = lambda a, b: jnp.concatenate([a, b], axis=0)

    for c in range(tt // CHUNK):
        rows = slice(c * CHUNK, (c + 1) * CHUNK)
        rt = [rt_ref[rows, cols[g]] for g in groups]
        at = [at_ref[rows, cols[g]] for g in groups]
        bd_v = [bdiag(vv_ref[rows, cols[g]]) for g in groups]
        lhs2 = [rowcat(at[g], rt[g]) for g in groups]
        gb = [_dot_nt(lhs2[g], bdiag(bt_ref[rows, cols[g]])) for g in groups]
        gk = [_dot_nt(lhs2[g], bdiag(kt_ref[rows, cols[g]])) for g in groups]
        a_ab = [jnp.where(strict, gb[g][:CHUNK], 0.0) for g in groups]
        a_rb = [jnp.where(incl, gb[g][CHUNK:], 0.0).astype(BF16) for g in groups]
        a_kk = [rowcat(jnp.where(strict, gk[g][:CHUNK], 0.0), jnp.where(incl, gk[g][CHUNK:], 0.0)).astype(BF16)
                for g in groups]
        pw_b = [a_ab[g].astype(BF16) for g in groups]
        tinv = [eye + a_ab[g] for g in groups]
        pw_b = [_dot(pw_b[g], bdiag(pw_b[g])).astype(BF16) for g in groups]
        for _ in range(N_DOUBLINGS - 1):
            res = [_dot(rowcat(pw_b[g], tinv[g].astype(BF16)), bdiag(pw_b[g])) for g in groups]
            pw_b = [res[g][:CHUNK].astype(BF16) for g in groups]
            tinv = [tinv[g] + res[g][CHUNK:] for g in groups]
        t_b = [(tinv[g] + _dot(tinv[g].astype(BF16), bdiag(pw_b[g]))).astype(BF16) for g in groups]
        av2 = [_dot(a_kk[g], bd_v[g]) for g in groups]
        w_a = [_dot(t_b[g], bdiag(at[g])).astype(BF16) for g in groups]
        u0 = [_dot(t_b[g], bdiag(av2[g][:CHUNK].astype(BF16))) for g in groups]
        for g in groups:
            wa_ref[rows, cols[g]] = w_a[g]
            u0_ref[rows, cols[g]] = u0[g]
            rp_ref[rows, cols[g]] = (rt[g].astype(F32) + _dot(a_rb[g], bdiag(w_a[g]))).astype(BF16)
            yacc_ref[rows, cols[g]] = _dot(a_rb[g], bdiag(u0[g].astype(BF16))) + av2[g][CHUNK:]

    for c in range(tt // CHUNK):
        rows = slice(c * CHUNK, (c + 1) * CHUNK)
        s_old = [s_ref[g] for g in groups]
        res = [_dot_nt(rowcat(wa_ref[rows, cols[g]], rp_ref[rows, cols[g]]), s_old[g].astype(BF16)) for g in groups]
        u_b = [(res[g][:CHUNK] + u0_ref[rows, cols[g]]).astype(BF16) for g in groups]
        upd = [_dot_tn(rowcat(u_b[g], vv_ref[rows, cols[g]]), rowcat(bh_ref[rows, cols[g]], kh_ref[rows, cols[g]]))
               for g in groups]
        for g in groups:
            yacc_ref[rows, cols[g]] = res[g][CHUNK:] + yacc_ref[rows, cols[g]]
            w_l = el_ref[c:c + 1, cols[g]]
            s_ref[g] = s_old[g] * w_l + jnp.where(hmask, upd[g], 0.0)

    y = yacc_ref[...]
    mean = headsum(y) * (1.0 / HEAD_DIM)
    d = y - mean
    var = headsum(d * d) * (1.0 / HEAD_DIM)
    yn = d * lax.rsqrt(var + GN_EPS) * lng_ref[...] + lnb_ref[...]
    y_ref[...] = ((yn + bonus_ref[...]) * gate_ref[...]).astype(BF16)

    @pl.when(t == pl.num_programs(1) - 1)
    def _():
        for g in range(N_GROUPS):
            s_g = s_ref[g]
            for h in range(GROUP_HEADS):
                lo = h * HEAD_DIM
                sout_ref[g * GROUP_HEADS + h] = s_g[lo:lo + HEAD_DIM, lo:lo + HEAD_DIM]


def _wkv(p3, shift0, s0, wts, tt, t_valid):
    b, t_total, _ = p3.shape
    idx = jnp.arange(tt)
    same_chunk = (idx[:, None] // CHUNK) == (idx[None, :] // CHUNK)
    tri = (same_chunk & (idx[None, :] <= idx[:, None])).astype(BF16)
    gidx = jnp.arange(GROUP_W) // HEAD_DIM
    hmask = (gidx[:, None] == gidx[None, :]).astype(BF16)

    def const(shape):
        nd = len(shape)
        return pl.BlockSpec(shape, lambda i, j, _nd=nd: (0,) * _nd)

    row = lambda w: const((1, w))
    kern = functools.partial(_wkv_kernel, tt=tt, t_valid=t_valid, t_total=t_total)
    act = lambda dt: pltpu.VMEM((tt, RWKV_DIM), dt)
    return pl.pallas_call(
        kern,
        grid=(b, t_total // tt),
        in_specs=[
            pl.BlockSpec((None, tt, P_COLS), lambda i, j: (i, j, 0)),
            pl.BlockSpec((None, 1, P_COLS), lambda i, j: (i, 0, 0)),
            pl.BlockSpec((None, N_HEADS, HEAD_DIM, HEAD_DIM), lambda i, j: (i, 0, 0, 0)),
            row(P_COLS), row(RWKV_DIM), const((LA_W, RWKV_DIM)), row(RWKV_DIM), const((LA_W, RWKV_DIM)),
            const((GL_W, RWKV_DIM)), row(RWKV_DIM), row(RWKV_DIM), row(RWKV_DIM), row(RWKV_DIM), row(RWKV_DIM),
            const((tt, tt)), const((GROUP_W, GROUP_W)),
        ],
        out_specs=[
            pl.BlockSpec((None, tt, RWKV_DIM), lambda i, j: (i, j, 0)),
            pl.BlockSpec((None, N_HEADS, HEAD_DIM, HEAD_DIM), lambda i, j: (i, 0, 0, 0)),
        ],
        out_shape=[
            jax.ShapeDtypeStruct((b, t_total, RWKV_DIM), BF16),
            jax.ShapeDtypeStruct((b, N_HEADS, HEAD_DIM, HEAD_DIM), F32),
        ],
        scratch_shapes=[
            pltpu.VMEM((N_GROUPS, GROUP_W, GROUP_W), F32),
            pltpu.VMEM((1, P_COLS), F32),
            act(BF16), act(BF16), act(BF16), act(BF16), act(BF16), act(BF16), act(BF16), act(BF16), act(BF16),
            pltpu.VMEM((max(tt // CHUNK, SUBLANES), RWKV_DIM), F32),
            act(F32), act(F32), act(F32), act(F32),
        ],
        compiler_params=_cparams(("arbitrary", "arbitrary")),
        name="wkv",
    )(p3, shift0, s0, wts["mu"], wts["w0"], wts["wdec"], wts["a0"], wts["waaa"], wts["wgate"],
      wts["k_k"], wts["k_a"], wts["r_k"], wts["lnx_g"], wts["lnx_b"], tri, hmask)


def _mixout_kernel(y_ref, c_ref, cache_ref, cw_ref, cb_ref, lg_ref, lb_ref, wo_ref, x_ref, o_ref,
                   cbuf_ref, *, tm):
    t = pl.program_id(1)

    @pl.when(t == 0)
    def _():
        cbuf_ref[0:CONV_HALO, :] = cache_ref[...]

    @pl.when(t > 0)
    def _():
        cbuf_ref[0:CONV_HALO, :] = cbuf_ref[tm:tm + CONV_HALO, :]

    cbuf_ref[CONV_HALO:CONV_HALO + tm, :] = c_ref[...]
    lead = CONV_HALO - (CONV_W - 1)
    crb = min(CONV_RB, tm)
    n_win = crb + CONV_HALO
    rep = lambda a: jnp.concatenate([a] * (crb // SUBLANES), axis=0)
    bias = jnp.broadcast_to(cb_ref[...], (crb, CONV_CH))
    gain = jnp.broadcast_to(lg_ref[...], (crb, CONV_CH))
    shift = jnp.broadcast_to(lb_ref[...], (crb, CONV_CH))
    acts = []
    for rb in range(tm // crb):
        base = rb * crb
        parts = []
        for lc in range(CONV_CH // CONV_LANES):
            lanes = slice(lc * CONV_LANES, (lc + 1) * CONV_LANES)
            win = cbuf_ref[base:base + n_win, lanes]
            acc = None
            for sub in range(SUBLANES):
                taps = [j for j in range(CONV_W) if (lead + j) % SUBLANES == sub]
                if not taps:
                    continue
                shifted = win if sub == 0 else pltpu.roll(win, n_win - sub, 0)
                for j in taps:
                    off = lead + j - sub
                    term = rep(cw_ref[j, :, lanes]) * shifted[off:off + crb, :]
                    acc = term if acc is None else acc + term
            parts.append(acc)
        acc = jnp.concatenate(parts, axis=1) + bias
        mean = jnp.mean(acc, axis=-1, keepdims=True)
        d = acc - mean
        var = jnp.mean(d * d, axis=-1, keepdims=True)
        yn = d * lax.rsqrt(var + LN_EPS) * gain + shift
        acts.append((yn * jax.nn.sigmoid(yn)).astype(BF16))

    cact = jnp.concatenate(acts, axis=0)
    mix = _dot(y_ref[...], wo_ref[0:RWKV_DIM, :]) + _dot(cact, wo_ref[RWKV_DIM:RWKV_DIM + CONV_CH, :])
    o_ref[...] = x_ref[...] + mix


def _mixout(y3, c3, cache, wts, x3, tm):
    b, t_total, _ = x3.shape

    def const(shape):
        nd = len(shape)
        return pl.BlockSpec(shape, lambda i, j, _nd=nd: (0,) * _nd)

    rows = lambda width: pl.BlockSpec((None, tm, width), lambda i, j: (i, j, 0))
    return pl.pallas_call(
        functools.partial(_mixout_kernel, tm=tm),
        grid=(b, t_total // tm),
        in_specs=[
            rows(RWKV_DIM),
            rows(CONV_CH),
            pl.BlockSpec((None, CONV_HALO, CONV_CH), lambda i, j: (i, 0, 0)),
            const((CONV_HALO, SUBLANES, CONV_CH)), const((1, CONV_CH)), const((1, CONV_CH)), const((1, CONV_CH)),
            const((RWKV_DIM + CONV_CH, D_MODEL)),
            rows(D_MODEL),
        ],
        out_specs=rows(D_MODEL),
        out_shape=jax.ShapeDtypeStruct((b, t_total, D_MODEL), F32),
        scratch_shapes=[pltpu.VMEM((tm + CONV_HALO, CONV_CH), F32)],
        compiler_params=_cparams(("arbitrary", "arbitrary")),
        name="mixout",
    )(y3, c3, cache, wts["conv_w"], wts["conv_b"], wts["conv_ln_g"], wts["conv_ln_b"], wts["w_out"], x3)


def _gelu_tanh(x):
    return 0.5 * x * (1.0 + jnp.tanh(math.sqrt(2.0 / math.pi) * (x + 0.044715 * (x * x * x))))


def _ffn_kernel(x_ref, gn_ref, wu_ref, wz_ref, fw_ref, fb_ref, wd_ref, cache_ref, fn_ref,
                o_ref, tail_ref, h_ref, *, tm, n_seq, seq_rows, tail_off):
    i = pl.program_id(1)
    j = pl.program_id(2)

    @pl.when((i == 0) & (j == 0))
    def _():
        tail_ref[...] = cache_ref[...]

    @pl.when(j == 0)
    def _():
        x = x_ref[...]
        ms = jnp.mean(x * x, axis=-1, keepdims=True)
        h_ref[...] = (x * lax.rsqrt(ms + RMS_EPS) * gn_ref[...]).astype(BF16)
        o_ref[...] = jnp.zeros_like(o_ref)

    h = h_ref[...]
    u = _dot(h, wu_ref[...])
    z = _dot(h, wz_ref[...])
    prev = tail_ref[j]
    row = lax.broadcasted_iota(jnp.int32, (tm, 1), 0)
    u_m1 = pltpu.roll(u, 1, 0)
    u_m2 = pltpu.roll(u, 2, 0)
    for s in range(n_seq):
        pm1 = prev[(s + 1) * SUBLANES - 1:(s + 1) * SUBLANES, :]
        pm2 = prev[(s + 1) * SUBLANES - 2:(s + 1) * SUBLANES - 1, :]
        u_m1 = jnp.where(row == s * seq_rows, pm1, u_m1)
        u_m2 = jnp.where(row == s * seq_rows, pm2, jnp.where(row == s * seq_rows + 1, pm1, u_m2))
    uc = fw_ref[0:1, :] * u_m2 + fw_ref[1:2, :] * u_m1 + fw_ref[2:3, :] * u + fb_ref[...]

    for s in range(n_seq):
        end = s * seq_rows + tail_off
        tail_ref[j, s * SUBLANES:(s + 1) * SUBLANES, :] = u[end - SUBLANES:end, :]

    act = (_gelu_tanh(uc) * z).astype(BF16)
    o_ref[...] += _dot(act, wd_ref[...])

    @pl.when(j == pl.num_programs(2) - 1)
    def _():
        x2 = x_ref[...] + o_ref[...]
        ms = jnp.mean(x2 * x2, axis=-1, keepdims=True)
        o_ref[...] = x2 * lax.rsqrt(ms + RMS_EPS) * fn_ref[...]


def _ffn(x3, cache, wts, tm, t_valid):
    b0, t0, _ = x3.shape
    n_seq = max(tm // t0, 1)
    assert b0 % n_seq == 0 and (tm % t0 == 0 or t0 % tm == 0)
    b, t_total = b0 // n_seq, t0 * n_seq
    x3 = x3.reshape(b, t_total, D_MODEL)
    n_i = t_total // tm
    assert t_valid == t0 or n_i == 1
    tail_off = t_valid if n_i == 1 else tm
    assert tail_off >= SUBLANES
    tail_rows = n_seq * SUBLANES
    cache = jnp.concatenate([jnp.zeros((b0, SUBLANES - (FFN_CONV_W - 1), D_FF), F32), cache.astype(F32)], axis=1)
    cache = cache.reshape(b, tail_rows, N_FF_TILES, FF_TILE).transpose(0, 2, 1, 3)

    def const(shape):
        nd = len(shape)
        return pl.BlockSpec(shape, lambda s, i, j, _nd=nd: (0,) * _nd)

    kern = functools.partial(_ffn_kernel, tm=tm, n_seq=n_seq, seq_rows=t0, tail_off=tail_off)
    y, tail = pl.pallas_call(
        kern,
        grid=(b, n_i, N_FF_TILES),
        in_specs=[
            pl.BlockSpec((None, tm, D_MODEL), lambda s, i, j: (s, i, 0), pipeline_mode=pl.Buffered(1)),
            const((1, D_MODEL)),
            pl.BlockSpec((D_MODEL, FF_TILE), lambda s, i, j: (0, j)),
            pl.BlockSpec((D_MODEL, FF_TILE), lambda s, i, j: (0, j + N_FF_TILES)),
            pl.BlockSpec((SUBLANES, FF_TILE), lambda s, i, j: (0, j)),
            pl.BlockSpec((1, FF_TILE), lambda s, i, j: (0, j)),
            pl.BlockSpec((FF_TILE, D_MODEL), lambda s, i, j: (j, 0)),
            pl.BlockSpec((None, N_FF_TILES, tail_rows, FF_TILE), lambda s, i, j: (s, 0, 0, 0)),
            const((1, D_MODEL)),
        ],
        out_specs=[
            pl.BlockSpec((None, tm, D_MODEL), lambda s, i, j: (s, i, 0)),
            pl.BlockSpec((None, N_FF_TILES, tail_rows, FF_TILE), lambda s, i, j: (s, 0, 0, 0)),
        ],
        out_shape=[
            jax.ShapeDtypeStruct((b, t_total, D_MODEL), F32),
            jax.ShapeDtypeStruct((b, N_FF_TILES, tail_rows, FF_TILE), F32),
        ],
        scratch_shapes=[pltpu.VMEM((tm, D_MODEL), BF16)],
        compiler_params=_cparams(("arbitrary", "arbitrary", "arbitrary")),
        name="ffn",
    )(x3, wts["norm_ffn"], wts["w_up"], wts["w_up"], wts["ffn_conv_w"], wts["ffn_conv_b"],
      wts["w_down"], cache, wts["final_norm"])
    new_cache = tail.transpose(0, 2, 1, 3).reshape(b0, SUBLANES, D_FF)[:, SUBLANES - (FFN_CONV_W - 1):]
    return y.reshape(b0, t0, D_MODEL), new_cache


def _pad_cols_rwkv(a):
    d3 = 3 * RWKV_DIM
    lead = a.shape[:-1]
    zeros = lambda w: jnp.zeros(lead + (w,), a.dtype)
    return jnp.concatenate([
        a[..., :d3 + LA_W], zeros(GL_OFF - LA_OFF - LA_W),
        a[..., d3 + LA_W:], zeros(GL_W - GATE_LORA)], axis=-1)


def _unpad_cols_rwkv(a):
    d3 = 3 * RWKV_DIM
    return jnp.concatenate([a[..., :d3 + LA_W], a[..., GL_OFF:GL_OFF + GATE_LORA]], axis=-1)


def _prep_weights(norm_mix, w_in, tshift_mu, w0, w_decay_up, a0, w_aaa_up, w_gate_up, k_k, k_a, r_k,
                  lnx_g, lnx_b, conv_w, conv_b, conv_ln_g, conv_ln_b, w_out, norm_ffn, w_ffn_up,
                  ffn_conv_w, ffn_conv_b, w_ffn_down, final_norm):
    row = lambda a: a.reshape(1, -1).astype(F32)
    w_in_b = w_in.astype(BF16)
    zrows = lambda n: jnp.zeros((n, RWKV_DIM), F32)

    return {
        "norm_mix": row(norm_mix),
        "w_rkv": w_in_b[:, :3 * RWKV_DIM],
        "w_lora": jnp.concatenate([
            w_in_b[:, LA_OFF:LA_OFF + LA_W], jnp.zeros((D_MODEL, GL_OFF - LA_OFF - LA_W), BF16),
            w_in_b[:, LA_OFF + LA_W:RWKV_COLS], jnp.zeros((D_MODEL, GL_W - GATE_LORA), BF16)], axis=1),
        "w_val": w_in_b[:, RWKV_COLS:RWKV_COLS + CONV_CH],
        "w_gate": w_in_b[:, RWKV_COLS + CONV_CH:],
        "mu": row(_pad_cols_rwkv(tshift_mu)),
        "w0": row(w0),
        "wdec": jnp.concatenate([w_decay_up, zrows(AAA_LORA)], axis=0).astype(BF16),
        "a0": row(a0),
        "waaa": jnp.concatenate([zrows(DECAY_LORA), w_aaa_up], axis=0).astype(BF16),
        "wgate": jnp.concatenate([w_gate_up, zrows(GL_W - GATE_LORA)], axis=0).astype(BF16),
        "k_k": row(k_k), "k_a": row(k_a), "r_k": row(r_k), "lnx_g": row(lnx_g), "lnx_b": row(lnx_b),
        "conv_w": jnp.broadcast_to(
            jnp.concatenate([conv_w, jnp.zeros((CONV_HALO - CONV_W, CONV_CH), F32)], axis=0).astype(F32)[:, None, :],
            (CONV_HALO, SUBLANES, CONV_CH)),
        "conv_b": row(conv_b), "conv_ln_g": row(conv_ln_g), "conv_ln_b": row(conv_ln_b),
        "w_out": w_out.astype(BF16),
        "norm_ffn": row(norm_ffn),
        "w_up": w_ffn_up.astype(BF16),
        "ffn_conv_w": jnp.concatenate([ffn_conv_w, jnp.zeros((SUBLANES - FFN_CONV_W, D_FF), F32)], axis=0).astype(F32),
        "ffn_conv_b": row(ffn_conv_b),
        "w_down": w_ffn_down.astype(BF16),
        "final_norm": row(final_norm),
    }


def _layer(x3, shift, wkv_state, conv_cache, ffn_cache, wts, tiles):
    b, t_valid, _ = x3.shape
    tm_in, tt, tm_mix, tm_ffn = tiles
    p2, c2 = _inproj(x3.reshape(b * t_valid, D_MODEL), wts["norm_mix"], wts, tm_in)
    p3 = p2.reshape(b, t_valid, P_COLS)
    c3 = c2.reshape(b, t_valid, CONV_CH)

    t_scan = -(-t_valid // CHUNK) * CHUNK
    p_scan = jnp.pad(p3, ((0, 0), (0, t_scan - t_valid), (0, 0)))
    y3, new_wkv = _wkv(p_scan, _pad_cols_rwkv(shift.astype(F32)), wkv_state.astype(F32), wts, tt, t_valid)
    y3 = y3[:, :t_valid]

    cache_pad = jnp.concatenate(
        [jnp.zeros((b, CONV_HALO - (CONV_W - 1), CONV_CH), F32), conv_cache.astype(F32)], axis=1)
    x1 = _mixout(y3, c3, cache_pad, wts, x3, tm_mix)

    y, new_ffn = _ffn(x1, ffn_cache, wts, tm_ffn, t_valid)

    new_shift = _unpad_cols_rwkv(p3[:, t_valid - 1:t_valid, :]).astype(F32)
    hist = CONV_W - 1
    if t_valid >= hist:
        new_conv = c3[:, t_valid - hist:t_valid]
    else:
        new_conv = jnp.concatenate([conv_cache.astype(F32)[:, t_valid:], c3[:, :t_valid]], axis=1)
    return y, new_shift, new_wkv, new_conv, new_ffn


def _prompt_tiles(b, t):
    tiles = (min(1024, b * t), min(256, t), min(512, t), min(1024, t))
    assert (b * t) % tiles[0] == 0 and all(t % x == 0 for x in tiles[1:]) and tiles[1] % CHUNK == 0
    return tiles


def kernel(x_prompt, x_sample, state_shift, state_wkv, cache_conv, cache_ffn_conv, meta, norm_mix, w_in,
           tshift_mu, w0, w_decay_up, a0, w_aaa_up, w_gate_up, k_k, k_a, r_k, lnx_g, lnx_b, conv_w, conv_b,
           conv_ln_g, conv_ln_b, w_out, norm_ffn, w_ffn_up, ffn_conv_w, ffn_conv_b, w_ffn_down, final_norm):
    depth = w_in.shape[0]
    bp, t_prompt, _ = x_prompt.shape
    bs, t_s, _ = x_sample.shape
    assert t_s == N_META, "the short streams (meta prefix, sample) share one padded pass"
    dt = x_prompt.dtype

    xs = jnp.concatenate([meta.astype(dt)[None], x_sample], axis=0)
    xp = x_prompt
    nb = bs + 1

    outs_p, outs_s = [], []
    for l in range(depth):
        wts = _prep_weights(norm_mix[l], w_in[l], tshift_mu[l], w0[l], w_decay_up[l], a0[l], w_aaa_up[l],
                            w_gate_up[l], k_k[l], k_a[l], r_k[l], lnx_g[l], lnx_b[l], conv_w[l], conv_b[l],
                            conv_ln_g[l], conv_ln_b[l], w_out[l], norm_ffn[l], w_ffn_up[l], ffn_conv_w[l],
                            ffn_conv_b[l], w_ffn_down[l], final_norm)
        z = lambda *s: jnp.zeros(s, F32)
        st_s = (
            jnp.concatenate([z(1, 1, RWKV_COLS), state_shift[l]], axis=0),
            jnp.concatenate([z(1, N_HEADS, HEAD_DIM, HEAD_DIM), state_wkv[l]], axis=0),
            jnp.concatenate([z(1, CONV_W - 1, CONV_CH), cache_conv[l]], axis=0),
            jnp.concatenate([z(1, FFN_CONV_W - 1, D_FF), cache_ffn_conv[l]], axis=0),
        )
        ys, sh_s, wk_s, cv_s, ff_s = _layer(xs, *st_s, wts, (nb * t_s, CHUNK, t_s, nb * t_s))
        bc = lambda a: jnp.broadcast_to(a[0:1], (bp,) + a.shape[1:])
        yp, sh_p, wk_p, cv_p, ff_p = _layer(xp, bc(sh_s), bc(wk_s), bc(cv_s), bc(ff_s), wts,
                                            _prompt_tiles(bp, t_prompt))
        outs_s.append((sh_s[1:], wk_s[1:], cv_s[1:], ff_s[1:]))
        outs_p.append((sh_p, wk_p, cv_p, ff_p))
        assert depth == 1
        xs, xp = ys, yp

    stack = lambda outs, i: jnp.stack([o[i] for o in outs]).astype(dt)
    y_prompt = xp
    y_sample = xs[1:]
    return (y_prompt, y_sample,
            stack(outs_p, 0), stack(outs_p, 1), stack(outs_p, 2), stack(outs_p, 3),
            stack(outs_s, 0), stack(outs_s, 1), stack(outs_s, 2), stack(outs_s, 3))
```

```python
import functools
import math

import jax
import jax.numpy as jnp
from jax import lax
from jax.experimental import pallas as pl
from jax.experimental.pallas import tpu as pltpu

F32 = jnp.float32
BF16 = jnp.bfloat16

D_MODEL = 2048
RWKV_DIM = 1024
HEAD_DIM = 64
N_HEADS = RWKV_DIM // HEAD_DIM
DECAY_LORA = 64
AAA_LORA = 64
GATE_LORA = 160
RWKV_COLS = 3 * RWKV_DIM + DECAY_LORA + AAA_LORA + GATE_LORA
CONV_CH = 1024
CONV_W = 31
D_FF = 5632
FFN_CONV_W = 3
N_META = 16
RMS_EPS = 1e-6
LN_EPS = 1e-5
GN_EPS = 1e-5 * HEAD_DIM

MXU_W = 256
SUBLANES = 8
VMEM_LIMIT_BYTES = 56 * 1024 * 1024

GROUP_W = MXU_W
GROUP_HEADS = GROUP_W // HEAD_DIM
N_GROUPS = RWKV_DIM // GROUP_W
CHUNK = MXU_W // GROUP_HEADS
N_DOUBLINGS = int(math.log2(CHUNK)) - 1

COL_TILE = 512
LORA_W = COL_TILE
LA_OFF = 3 * RWKV_DIM
LA_W = DECAY_LORA + AAA_LORA
GL_OFF = LA_OFF + MXU_W
GL_W = MXU_W
P_COLS = 3 * RWKV_DIM + LORA_W
N_RKV_TILES = 3 * RWKV_DIM // COL_TILE
N_P_TILES = P_COLS // COL_TILE
GLU_W = COL_TILE // 2
N_GLU_TILES = CONV_CH // GLU_W
CONV_HALO = 32
CONV_RB = 32
CONV_LANES = 256
FF_TILE = 512
N_FF_TILES = D_FF // FF_TILE


def _cparams(sem):
    return pltpu.CompilerParams(dimension_semantics=sem, vmem_limit_bytes=VMEM_LIMIT_BYTES)


def _dot(a, b):
    return jnp.dot(a, b, preferred_element_type=F32)


def _dot_nt(a, b):
    return lax.dot_general(a, b, (((1,), (1,)), ((), ())), preferred_element_type=F32)


def _dot_tn(a, b):
    return lax.dot_general(a, b, (((0,), (0,)), ((), ())), preferred_element_type=F32)


def _inproj_kernel(x_ref, g_ref, wrkv_ref, wlora_ref, wval_ref, wgate_ref, p_ref, c_ref, h_ref):
    j = pl.program_id(1)

    @pl.when(j == 0)
    def _():
        x = x_ref[...]
        ms = jnp.mean(x * x, axis=-1, keepdims=True)
        h_ref[...] = (x * lax.rsqrt(ms + RMS_EPS) * g_ref[...]).astype(BF16)

    @pl.when(j < N_RKV_TILES)
    def _():
        p_ref[...] = _dot(h_ref[...], wrkv_ref[...]).astype(p_ref.dtype)

    @pl.when(j == N_RKV_TILES)
    def _():
        p_ref[...] = _dot(h_ref[...], wlora_ref[...]).astype(p_ref.dtype)

    @pl.when(j >= N_P_TILES)
    def _():
        h = h_ref[...]
        c_ref[...] = _dot(h, wval_ref[...]) * jax.nn.sigmoid(_dot(h, wgate_ref[...]))


def _inproj(x2d, g, wts, tm):
    rows = x2d.shape[0]
    n_col = N_P_TILES + N_GLU_TILES
    glu_tile = lambda i, j: (0, jnp.maximum(j - N_P_TILES, 0))
    return pl.pallas_call(
        _inproj_kernel,
        grid=(rows // tm, n_col),
        in_specs=[
            pl.BlockSpec((tm, D_MODEL), lambda i, j: (i, 0)),
            pl.BlockSpec((1, D_MODEL), lambda i, j: (0, 0)),
            pl.BlockSpec((D_MODEL, COL_TILE), lambda i, j: (0, jnp.minimum(j, N_RKV_TILES - 1))),
            pl.BlockSpec((D_MODEL, LORA_W), lambda i, j: (0, 0)),
            pl.BlockSpec((D_MODEL, GLU_W), glu_tile),
            pl.BlockSpec((D_MODEL, GLU_W), glu_tile),
        ],
        out_specs=[
            pl.BlockSpec((tm, COL_TILE), lambda i, j: (i, jnp.minimum(j, N_P_TILES - 1))),
            pl.BlockSpec((tm, GLU_W), lambda i, j: (i, jnp.maximum(j - N_P_TILES, 0))),
        ],
        out_shape=[
            jax.ShapeDtypeStruct((rows, P_COLS), BF16),
            jax.ShapeDtypeStruct((rows, CONV_CH), F32),
        ],
        scratch_shapes=[pltpu.VMEM((tm, D_MODEL), BF16)],
        compiler_params=_cparams(("arbitrary", "arbitrary")),
        name="inproj",
    )(x2d, g, wts["w_rkv"], wts["w_lora"], wts["w_val"], wts["w_gate"])


def _wkv_kernel(p_ref, shift0_ref, s0_ref, mu_ref, w0_ref, wdec_ref, a0_ref, waaa_ref, wgate_ref,
                kk_ref, ka_ref, rk_ref, lng_ref, lnb_ref, tri_ref, hmask_ref,
                y_ref, sout_ref,
                s_ref, prev_ref, rt_ref, at_ref, kt_ref, bt_ref, kh_ref, bh_ref, vv_ref, wa_ref, rp_ref, el_ref,
                yacc_ref, bonus_ref, gate_ref, u0_ref, *, tt, t_valid, t_total):
    t = pl.program_id(1)

    @pl.when(t == 0)
    def _():
        prev_ref[...] = shift0_ref[...]
        zero = jnp.zeros((HEAD_DIM, HEAD_DIM), F32)
        for g in range(N_GROUPS):
            band = [jnp.concatenate([s0_ref[g * GROUP_HEADS + h] if hh == h else zero for hh in range(GROUP_HEADS)],
                                    axis=1) for h in range(GROUP_HEADS)]
            s_ref[g] = jnp.concatenate(band, axis=0)

    hmask_bf = hmask_ref[...]
    head_shift = HEAD_DIM.bit_length() - 1
    hmask = (jnp.right_shift(lax.broadcasted_iota(jnp.int32, (GROUP_W, GROUP_W), 0), head_shift)
             == jnp.right_shift(lax.broadcasted_iota(jnp.int32, (GROUP_W, GROUP_W), 1), head_shift))

    def headsum(x):
        parts = [_dot(x[:, g * GROUP_W:(g + 1) * GROUP_W].astype(BF16), hmask_bf) for g in range(N_GROUPS)]
        return jnp.concatenate(parts, axis=1)

    p = p_ref[...].astype(F32)
    row = lax.broadcasted_iota(jnp.int32, (tt, 1), 0)
    p_prev = jnp.where(row == 0, prev_ref[...], pltpu.roll(p, 1, 0))
    prev_ref[...] = p[tt - 1:tt, :]
    ps = p + (p_prev - p) * mu_ref[...]
    r = ps[:, 0:RWKV_DIM]
    k = ps[:, RWKV_DIM:2 * RWKV_DIM]
    v = ps[:, 2 * RWKV_DIM:3 * RWKV_DIM]
    la = ps[:, LA_OFF:LA_OFF + LA_W]
    gl = ps[:, GL_OFF:GL_OFF + GL_W]

    z = w0_ref[...] + _dot(jnp.tanh(la).astype(BF16), wdec_ref[...])
    lw = -math.exp(-0.5) * jax.nn.sigmoid(z)
    a_lr = jax.nn.sigmoid(a0_ref[...] + _dot(la.astype(BF16), waaa_ref[...]))
    gate_ref[...] = _dot(jax.nn.sigmoid(gl).astype(BF16), wgate_ref[...])

    kk = k * kk_ref[...]
    kkn = kk * lax.rsqrt(jnp.maximum(headsum(kk * kk), 1e-24))
    k2 = k * (1.0 + (a_lr - 1.0) * ka_ref[...])
    bvec = kkn * a_lr
    avec = -kkn
    bonus_ref[...] = headsum(r * k2 * rk_ref[...]) * v

    if t_valid < t_total:
        valid = (t * tt + row) < t_valid
        lw = jnp.where(valid, lw, 0.0)
        k2 = jnp.where(valid, k2, 0.0)
        bvec = jnp.where(valid, bvec, 0.0)
        avec = jnp.where(valid, avec, 0.0)

    lw_hi = lw.astype(BF16)
    lw_lo = (lw - lw_hi.astype(F32)).astype(BF16)
    tri = tri_ref[...]
    cs = _dot(tri, lw_hi) + _dot(tri, lw_lo)
    ec = jnp.exp(cs)
    emc = jnp.exp(-cs)
    kt = k2 * emc
    bt = bvec * emc
    rt_ref[...] = (r * ec).astype(BF16)
    at_ref[...] = (avec * jnp.exp(cs - lw)).astype(BF16)
    kt_ref[...] = kt.astype(BF16)
    bt_ref[...] = bt.astype(BF16)
    vv_ref[...] = v.astype(BF16)
    for c in range(tt // CHUNK):
        rows = slice(c * CHUNK, (c + 1) * CHUNK)
        el = ec[(c + 1) * CHUNK - 1:(c + 1) * CHUNK, :]
        el_ref[c:c + 1, :] = el
        kh_ref[rows, :] = (kt[rows, :] * el).astype(BF16)
        bh_ref[rows, :] = (bt[rows, :] * el).astype(BF16)

    t_idx = lax.broadcasted_iota(jnp.int32, (CHUNK, GROUP_W), 0)
    s_idx = jnp.bitwise_and(lax.broadcasted_iota(jnp.int32, (CHUNK, GROUP_W), 1), CHUNK - 1)
    strict = t_idx > s_idx
    incl = t_idx >= s_idx
    eye = (t_idx == s_idx).astype(F32)

    def bdiag(zb):
        return jnp.concatenate([zb] * GROUP_HEADS, axis=0) * hmask_bf

    groups = range(N_GROUPS)
    cols = [slice(g * GROUP_W, (g + 1) * GROUP_W) for g in groups]
    rowcat = lambda a, b: jnp.concatenate([a, b], axis=0)

    pairs = [(c, g) for c in range(tt // CHUNK) for g in groups]
    idx = range(len(pairs))
    rws = [slice(c * CHUNK, (c + 1) * CHUNK) for c, _ in pairs]
    cls = [cols[g] for _, g in pairs]
    rt = [rt_ref[rws[i], cls[i]] for i in idx]
    at = [at_ref[rws[i], cls[i]] for i in idx]
    bd_v = [bdiag(vv_ref[rws[i], cls[i]]) for i in idx]
    lhs2 = [rowcat(at[i], rt[i]) for i in idx]
    gb = [_dot_nt(lhs2[i], bdiag(bt_ref[rws[i], cls[i]])) for i in idx]
    gk = [_dot_nt(lhs2[i], bdiag(kt_ref[rws[i], cls[i]])) for i in idx]
    a_ab = [jnp.where(strict, gb[i][:CHUNK], 0.0) for i in idx]
    a_rb = [jnp.where(incl, gb[i][CHUNK:], 0.0).astype(BF16) for i in idx]
    a_kk = [rowcat(jnp.where(strict, gk[i][:CHUNK], 0.0), jnp.where(incl, gk[i][CHUNK:], 0.0)).astype(BF16)
            for i in idx]
    pw_b = [a_ab[i].astype(BF16) for i in idx]
    tinv = [eye + a_ab[i] for i in idx]
    pw_b = [_dot(pw_b[i], bdiag(pw_b[i])).astype(BF16) for i in idx]
    for _ in range(N_DOUBLINGS - 1):
        res = [_dot(rowcat(pw_b[i], tinv[i].astype(BF16)), bdiag(pw_b[i])) for i in idx]
        pw_b = [res[i][:CHUNK].astype(BF16) for i in idx]
        tinv = [tinv[i] + res[i][CHUNK:] for i in idx]
    t_b = [(tinv[i] + _dot(tinv[i].astype(BF16), bdiag(pw_b[i]))).astype(BF16) for i in idx]
    av2 = [_dot(a_kk[i], bd_v[i]) for i in idx]
    w_a = [_dot(t_b[i], bdiag(at[i])).astype(BF16) for i in idx]
    u0 = [_dot(t_b[i], bdiag(av2[i][:CHUNK].astype(BF16))) for i in idx]
    for i in idx:
        wa_ref[rws[i], cls[i]] = w_a[i]
        u0_ref[rws[i], cls[i]] = u0[i]
        rp_ref[rws[i], cls[i]] = (rt[i].astype(F32) + _dot(a_rb[i], bdiag(w_a[i]))).astype(BF16)
        yacc_ref[rws[i], cls[i]] = _dot(a_rb[i], bdiag(u0[i].astype(BF16))) + av2[i][CHUNK:]

    for c in range(tt // CHUNK):
        rows = slice(c * CHUNK, (c + 1) * CHUNK)
        s_old = [s_ref[g] for g in groups]
        res = [_dot_nt(rowcat(wa_ref[rows, cols[g]], rp_ref[rows, cols[g]]), s_old[g].astype(BF16)) for g in groups]
        u_b = [(res[g][:CHUNK] + u0_ref[rows, cols[g]]).astype(BF16) for g in groups]
        upd = [_dot_tn(rowcat(u_b[g], vv_ref[rows, cols[g]]), rowcat(bh_ref[rows, cols[g]], kh_ref[rows, cols[g]]))
               for g in groups]
        for g in groups:
            yacc_ref[rows, cols[g]] = res[g][CHUNK:] + yacc_ref[rows, cols[g]]
            w_l = el_ref[c:c + 1, cols[g]]
            s_ref[g] = s_old[g] * w_l + jnp.where(hmask, upd[g], 0.0)

    y = yacc_ref[...]
    mean = headsum(y) * (1.0 / HEAD_DIM)
    d = y - mean
    var = headsum(d * d) * (1.0 / HEAD_DIM)
    yn = d * lax.rsqrt(var + GN_EPS) * lng_ref[...] + lnb_ref[...]
    y_ref[...] = ((yn + bonus_ref[...]) * gate_ref[...]).astype(BF16)

    @pl.when(t == pl.num_programs(1) - 1)
    def _():
        for g in range(N_GROUPS):
            s_g = s_ref[g]
            for h in range(GROUP_HEADS):
                lo = h * HEAD_DIM
                sout_ref[g * GROUP_HEADS + h] = s_g[lo:lo + HEAD_DIM, lo:lo + HEAD_DIM]


def _wkv(p3, shift0, s0, wts, tt, t_valid):
    b, t_total, _ = p3.shape
    idx = jnp.arange(tt)
    same_chunk = (idx[:, None] // CHUNK) == (idx[None, :] // CHUNK)
    tri = (same_chunk & (idx[None, :] <= idx[:, None])).astype(BF16)
    gidx = jnp.arange(GROUP_W) // HEAD_DIM
    hmask = (gidx[:, None] == gidx[None, :]).astype(BF16)

    def const(shape):
        nd = len(shape)
        return pl.BlockSpec(shape, lambda i, j, _nd=nd: (0,) * _nd)

    row = lambda w: const((1, w))
    kern = functools.partial(_wkv_kernel, tt=tt, t_valid=t_valid, t_total=t_total)
    act = lambda dt: pltpu.VMEM((tt, RWKV_DIM), dt)
    return pl.pallas_call(
        kern,
        grid=(b, t_total // tt),
        in_specs=[
            pl.BlockSpec((None, tt, P_COLS), lambda i, j: (i, j, 0)),
            pl.BlockSpec((None, 1, P_COLS), lambda i, j: (i, 0, 0)),
            pl.BlockSpec((None, N_HEADS, HEAD_DIM, HEAD_DIM), lambda i, j: (i, 0, 0, 0)),
            row(P_COLS), row(RWKV_DIM), const((LA_W, RWKV_DIM)), row(RWKV_DIM), const((LA_W, RWKV_DIM)),
            const((GL_W, RWKV_DIM)), row(RWKV_DIM), row(RWKV_DIM), row(RWKV_DIM), row(RWKV_DIM), row(RWKV_DIM),
            const((tt, tt)), const((GROUP_W, GROUP_W)),
        ],
        out_specs=[
            pl.BlockSpec((None, tt, RWKV_DIM), lambda i, j: (i, j, 0)),
            pl.BlockSpec((None, N_HEADS, HEAD_DIM, HEAD_DIM), lambda i, j: (i, 0, 0, 0)),
        ],
        out_shape=[
            jax.ShapeDtypeStruct((b, t_total, RWKV_DIM), BF16),
            jax.ShapeDtypeStruct((b, N_HEADS, HEAD_DIM, HEAD_DIM), F32),
        ],
        scratch_shapes=[
            pltpu.VMEM((N_GROUPS, GROUP_W, GROUP_W), F32),
            pltpu.VMEM((1, P_COLS), F32),
            act(BF16), act(BF16), act(BF16), act(BF16), act(BF16), act(BF16), act(BF16), act(BF16), act(BF16),
            pltpu.VMEM((max(tt // CHUNK, SUBLANES), RWKV_DIM), F32),
            act(F32), act(F32), act(F32), act(F32),
        ],
        compiler_params=_cparams(("arbitrary", "arbitrary")),
        name="wkv",
    )(p3, shift0, s0, wts["mu"], wts["w0"], wts["wdec"], wts["a0"], wts["waaa"], wts["wgate"],
      wts["k_k"], wts["k_a"], wts["r_k"], wts["lnx_g"], wts["lnx_b"], tri, hmask)


def _mixout_kernel(y_ref, c_ref, cache_ref, cw_ref, cb_ref, lg_ref, lb_ref, wo_ref, x_ref, o_ref,
                   cbuf_ref, *, tm):
    t = pl.program_id(1)

    @pl.when(t == 0)
    def _():
        cbuf_ref[0:CONV_HALO, :] = cache_ref[...]

    @pl.when(t > 0)
    def _():
        cbuf_ref[0:CONV_HALO, :] = cbuf_ref[tm:tm + CONV_HALO, :]

    cbuf_ref[CONV_HALO:CONV_HALO + tm, :] = c_ref[...]
    lead = CONV_HALO - (CONV_W - 1)
    crb = min(CONV_RB, tm)
    n_win = crb + CONV_HALO
    rep = lambda a: jnp.concatenate([a] * (crb // SUBLANES), axis=0)
    bias = jnp.broadcast_to(cb_ref[...], (crb, CONV_CH))
    gain = jnp.broadcast_to(lg_ref[...], (crb, CONV_CH))
    shift = jnp.broadcast_to(lb_ref[...], (crb, CONV_CH))
    acts = []
    for rb in range(tm // crb):
        base = rb * crb
        parts = []
        for lc in range(CONV_CH // CONV_LANES):
            lanes = slice(lc * CONV_LANES, (lc + 1) * CONV_LANES)
            win = cbuf_ref[base:base + n_win, lanes]
            acc = None
            for sub in range(SUBLANES):
                taps = [j for j in range(CONV_W) if (lead + j) % SUBLANES == sub]
                if not taps:
                    continue
                shifted = win if sub == 0 else pltpu.roll(win, n_win - sub, 0)
                for j in taps:
                    off = lead + j - sub
                    term = rep(cw_ref[j, :, lanes]) * shifted[off:off + crb, :]
                    acc = term if acc is None else acc + term
            parts.append(acc)
        acc = jnp.concatenate(parts, axis=1) + bias
        mean = jnp.mean(acc, axis=-1, keepdims=True)
        d = acc - mean
        var = jnp.mean(d * d, axis=-1, keepdims=True)
        yn = d * lax.rsqrt(var + LN_EPS) * gain + shift
        acts.append((yn * jax.nn.sigmoid(yn)).astype(BF16))

    cact = jnp.concatenate(acts, axis=0)
    mix = _dot(y_ref[...], wo_ref[0:RWKV_DIM, :]) + _dot(cact, wo_ref[RWKV_DIM:RWKV_DIM + CONV_CH, :])
    o_ref[...] = x_ref[...] + mix


def _mixout(y3, c3, cache, wts, x3, tm):
    b, t_total, _ = x3.shape

    def const(shape):
        nd = len(shape)
        return pl.BlockSpec(shape, lambda i, j, _nd=nd: (0,) * _nd)

    rows = lambda width: pl.BlockSpec((None, tm, width), lambda i, j: (i, j, 0))
    return pl.pallas_call(
        functools.partial(_mixout_kernel, tm=tm),
        grid=(b, t_total // tm),
        in_specs=[
            rows(RWKV_DIM),
            rows(CONV_CH),
            pl.BlockSpec((None, CONV_HALO, CONV_CH), lambda i, j: (i, 0, 0)),
            const((CONV_HALO, SUBLANES, CONV_CH)), const((1, CONV_CH)), const((1, CONV_CH)), const((1, CONV_CH)),
            const((RWKV_DIM + CONV_CH, D_MODEL)),
            rows(D_MODEL),
        ],
        out_specs=rows(D_MODEL),
        out_shape=jax.ShapeDtypeStruct((b, t_total, D_MODEL), F32),
        scratch_shapes=[pltpu.VMEM((tm + CONV_HALO, CONV_CH), F32)],
        compiler_params=_cparams(("arbitrary", "arbitrary")),
        name="mixout",
    )(y3, c3, cache, wts["conv_w"], wts["conv_b"], wts["conv_ln_g"], wts["conv_ln_b"], wts["w_out"], x3)


def _gelu_tanh(x):
    return 0.5 * x * (1.0 + jnp.tanh(math.sqrt(2.0 / math.pi) * (x + 0.044715 * (x * x * x))))


def _ffn_kernel(x_ref, gn_ref, wu_ref, wz_ref, fw_ref, fb_ref, wd_ref, cache_ref, fn_ref,
                o_ref, tail_ref, h_ref, *, tm, n_seq, seq_rows, tail_off):
    i = pl.program_id(1)
    j = pl.program_id(2)

    @pl.when((i == 0) & (j == 0))
    def _():
        tail_ref[...] = cache_ref[...]

    @pl.when(j == 0)
    def _():
        x = x_ref[...]
        ms = jnp.mean(x * x, axis=-1, keepdims=True)
        h_ref[...] = (x * lax.rsqrt(ms + RMS_EPS) * gn_ref[...]).astype(BF16)
        o_ref[...] = jnp.zeros_like(o_ref)

    h = h_ref[...]
    u = _dot(h, wu_ref[...])
    z = _dot(h, wz_ref[...])
    prev = tail_ref[j]
    row = lax.broadcasted_iota(jnp.int32, (tm, 1), 0)
    u_m1 = pltpu.roll(u, 1, 0)
    u_m2 = pltpu.roll(u, 2, 0)
    for s in range(n_seq):
        pm1 = prev[(s + 1) * SUBLANES - 1:(s + 1) * SUBLANES, :]
        pm2 = prev[(s + 1) * SUBLANES - 2:(s + 1) * SUBLANES - 1, :]
        u_m1 = jnp.where(row == s * seq_rows, pm1, u_m1)
        u_m2 = jnp.where(row == s * seq_rows, pm2, jnp.where(row == s * seq_rows + 1, pm1, u_m2))
    uc = fw_ref[0:1, :] * u_m2 + fw_ref[1:2, :] * u_m1 + fw_ref[2:3, :] * u + fb_ref[...]

    for s in range(n_seq):
        end = s * seq_rows + tail_off
        tail_ref[j, s * SUBLANES:(s + 1) * SUBLANES, :] = u[end - SUBLANES:end, :]

    act = (_gelu_tanh(uc) * z).astype(BF16)
    o_ref[...] += _dot(act, wd_ref[...])

    @pl.when(j == pl.num_programs(2) - 1)
    def _():
        x2 = x_ref[...] + o_ref[...]
        ms = jnp.mean(x2 * x2, axis=-1, keepdims=True)
        o_ref[...] = x2 * lax.rsqrt(ms + RMS_EPS) * fn_ref[...]


def _ffn(x3, cache, wts, tm, t_valid):
    b0, t0, _ = x3.shape
    n_seq = max(tm // t0, 1)
    assert b0 % n_seq == 0 and (tm % t0 == 0 or t0 % tm == 0)
    b, t_total = b0 // n_seq, t0 * n_seq
    x3 = x3.reshape(b, t_total, D_MODEL)
    n_i = t_total // tm
    assert t_valid == t0 or n_i == 1
    tail_off = t_valid if n_i == 1 else tm
    assert tail_off >= SUBLANES
    tail_rows = n_seq * SUBLANES
    cache = jnp.concatenate([jnp.zeros((b0, SUBLANES - (FFN_CONV_W - 1), D_FF), F32), cache.astype(F32)], axis=1)
    cache = cache.reshape(b, tail_rows, N_FF_TILES, FF_TILE).transpose(0, 2, 1, 3)

    def const(shape):
        nd = len(shape)
        return pl.BlockSpec(shape, lambda s, i, j, _nd=nd: (0,) * _nd)

    kern = functools.partial(_ffn_kernel, tm=tm, n_seq=n_seq, seq_rows=t0, tail_off=tail_off)
    y, tail = pl.pallas_call(
        kern,
        grid=(b, n_i, N_FF_TILES),
        in_specs=[
            pl.BlockSpec((None, tm, D_MODEL), lambda s, i, j: (s, i, 0), pipeline_mode=pl.Buffered(1)),
            const((1, D_MODEL)),
            pl.BlockSpec((D_MODEL, FF_TILE), lambda s, i, j: (0, j)),
            pl.BlockSpec((D_MODEL, FF_TILE), lambda s, i, j: (0, j + N_FF_TILES)),
            pl.BlockSpec((SUBLANES, FF_TILE), lambda s, i, j: (0, j)),
            pl.BlockSpec((1, FF_TILE), lambda s, i, j: (0, j)),
            pl.BlockSpec((FF_TILE, D_MODEL), lambda s, i, j: (j, 0)),
            pl.BlockSpec((None, N_FF_TILES, tail_rows, FF_TILE), lambda s, i, j: (s, 0, 0, 0)),
            const((1, D_MODEL)),
        ],
        out_specs=[
            pl.BlockSpec((None, tm, D_MODEL), lambda s, i, j: (s, i, 0)),
            pl.BlockSpec((None, N_FF_TILES, tail_rows, FF_TILE), lambda s, i, j: (s, 0, 0, 0)),
        ],
        out_shape=[
            jax.ShapeDtypeStruct((b, t_total, D_MODEL), F32),
            jax.ShapeDtypeStruct((b, N_FF_TILES, tail_rows, FF_TILE), F32),
        ],
        scratch_shapes=[pltpu.VMEM((tm, D_MODEL), BF16)],
        compiler_params=_cparams(("arbitrary", "arbitrary", "arbitrary")),
        name="ffn",
    )(x3, wts["norm_ffn"], wts["w_up"], wts["w_up"], wts["ffn_conv_w"], wts["ffn_conv_b"],
      wts["w_down"], cache, wts["final_norm"])
    new_cache = tail.transpose(0, 2, 1, 3).reshape(b0, SUBLANES, D_FF)[:, SUBLANES - (FFN_CONV_W - 1):]
    return y.reshape(b0, t0, D_MODEL), new_cache


def _pad_cols_rwkv(a):
    d3 = 3 * RWKV_DIM
    lead = a.shape[:-1]
    zeros = lambda w: jnp.zeros(lead + (w,), a.dtype)
    return jnp.concatenate([
        a[..., :d3 + LA_W], zeros(GL_OFF - LA_OFF - LA_W),
        a[..., d3 + LA_W:], zeros(GL_W - GATE_LORA)], axis=-1)


def _unpad_cols_rwkv(a):
    d3 = 3 * RWKV_DIM
    return jnp.concatenate([a[..., :d3 + LA_W], a[..., GL_OFF:GL_OFF + GATE_LORA]], axis=-1)


def _prep_weights(norm_mix, w_in, tshift_mu, w0, w_decay_up, a0, w_aaa_up, w_gate_up, k_k, k_a, r_k,
                  lnx_g, lnx_b, conv_w, conv_b, conv_ln_g, conv_ln_b, w_out, norm_ffn, w_ffn_up,
                  ffn_conv_w, ffn_conv_b, w_ffn_down, final_norm):
    row = lambda a: a.reshape(1, -1).astype(F32)
    w_in_b = w_in.astype(BF16)
    zrows = lambda n: jnp.zeros((n, RWKV_DIM), F32)

    return {
        "norm_mix": row(norm_mix),
        "w_rkv": w_in_b[:, :3 * RWKV_DIM],
        "w_lora": jnp.concatenate([
            w_in_b[:, LA_OFF:LA_OFF + LA_W], jnp.zeros((D_MODEL, GL_OFF - LA_OFF - LA_W), BF16),
            w_in_b[:, LA_OFF + LA_W:RWKV_COLS], jnp.zeros((D_MODEL, GL_W - GATE_LORA), BF16)], axis=1),
        "w_val": w_in_b[:, RWKV_COLS:RWKV_COLS + CONV_CH],
        "w_gate": w_in_b[:, RWKV_COLS + CONV_CH:],
        "mu": row(_pad_cols_rwkv(tshift_mu)),
        "w0": row(w0),
        "wdec": jnp.concatenate([w_decay_up, zrows(AAA_LORA)], axis=0).astype(BF16),
        "a0": row(a0),
        "waaa": jnp.concatenate([zrows(DECAY_LORA), w_aaa_up], axis=0).astype(BF16),
        "wgate": jnp.concatenate([w_gate_up, zrows(GL_W - GATE_LORA)], axis=0).astype(BF16),
        "k_k": row(k_k), "k_a": row(k_a), "r_k": row(r_k), "lnx_g": row(lnx_g), "lnx_b": row(lnx_b),
        "conv_w": jnp.broadcast_to(
            jnp.concatenate([conv_w, jnp.zeros((CONV_HALO - CONV_W, CONV_CH), F32)], axis=0).astype(F32)[:, None, :],
            (CONV_HALO, SUBLANES, CONV_CH)),
        "conv_b": row(conv_b), "conv_ln_g": row(conv_ln_g), "conv_ln_b": row(conv_ln_b),
        "w_out": w_out.astype(BF16),
        "norm_ffn": row(norm_ffn),
        "w_up": w_ffn_up.astype(BF16),
        "ffn_conv_w": jnp.concatenate([ffn_conv_w, jnp.zeros((SUBLANES - FFN_CONV_W, D_FF), F32)], axis=0).astype(F32),
        "ffn_conv_b": row(ffn_conv_b),
        "w_down": w_ffn_down.astype(BF16),
        "final_norm": row(final_norm),
    }


def _layer(x3, shift, wkv_state, conv_cache, ffn_cache, wts, tiles):
    b, t_valid, _ = x3.shape
    tm_in, tt, tm_mix, tm_ffn = tiles
    p2, c2 = _inproj(x3.reshape(b * t_valid, D_MODEL), wts["norm_mix"], wts, tm_in)
    p3 = p2.reshape(b, t_valid, P_COLS)
    c3 = c2.reshape(b, t_valid, CONV_CH)

    t_scan = -(-t_valid // CHUNK) * CHUNK
    p_scan = jnp.pad(p3, ((0, 0), (0, t_scan - t_valid), (0, 0)))
    y3, new_wkv = _wkv(p_scan, _pad_cols_rwkv(shift.astype(F32)), wkv_state.astype(F32), wts, tt, t_valid)
    y3 = y3[:, :t_valid]

    cache_pad = jnp.concatenate(
        [jnp.zeros((b, CONV_HALO - (CONV_W - 1), CONV_CH), F32), conv_cache.astype(F32)], axis=1)
    x1 = _mixout(y3, c3, cache_pad, wts, x3, tm_mix)

    y, new_ffn = _ffn(x1, ffn_cache, wts, tm_ffn, t_valid)

    new_shift = _unpad_cols_rwkv(p3[:, t_valid - 1:t_valid, :]).astype(F32)
    hist = CONV_W - 1
    if t_valid >= hist:
        new_conv = c3[:, t_valid - hist:t_valid]
    else:
        new_conv = jnp.concatenate([conv_cache.astype(F32)[:, t_valid:], c3[:, :t_valid]], axis=1)
    return y, new_shift, new_wkv, new_conv, new_ffn


def _prompt_tiles(b, t):
    tiles = (min(1024, b * t), min(256, t), min(512, t), min(1024, t))
    assert (b * t) % tiles[0] == 0 and all(t % x == 0 for x in tiles[1:]) and tiles[1] % CHUNK == 0
    return tiles


def kernel(x_prompt, x_sample, state_shift, state_wkv, cache_conv, cache_ffn_conv, meta, norm_mix, w_in,
           tshift_mu, w0, w_decay_up, a0, w_aaa_up, w_gate_up, k_k, k_a, r_k, lnx_g, lnx_b, conv_w, conv_b,
           conv_ln_g, conv_ln_b, w_out, norm_ffn, w_ffn_up, ffn_conv_w, ffn_conv_b, w_ffn_down, final_norm):
    depth = w_in.shape[0]
    bp, t_prompt, _ = x_prompt.shape
    bs, t_s, _ = x_sample.shape
    assert t_s == N_META, "the short streams (meta prefix, sample) share one padded pass"
    dt = x_prompt.dtype

    xs = jnp.concatenate([meta.astype(dt)[None], x_sample], axis=0)
    xp = x_prompt
    nb = bs + 1

    outs_p, outs_s = [], []
    for l in range(depth):
        wts = _prep_weights(norm_mix[l], w_in[l], tshift_mu[l], w0[l], w_decay_up[l], a0[l], w_aaa_up[l],
                            w_gate_up[l], k_k[l], k_a[l], r_k[l], lnx_g[l], lnx_b[l], conv_w[l], conv_b[l],
                            conv_ln_g[l], conv_ln_b[l], w_out[l], norm_ffn[l], w_ffn_up[l], ffn_conv_w[l],
                            ffn_conv_b[l], w_ffn_down[l], final_norm)
        z = lambda *s: jnp.zeros(s, F32)
        st_s = (
            jnp.concatenate([z(1, 1, RWKV_COLS), state_shift[l]], axis=0),
            jnp.concatenate([z(1, N_HEADS, HEAD_DIM, HEAD_DIM), state_wkv[l]], axis=0),
            jnp.concatenate([z(1, CONV_W - 1, CONV_CH), cache_conv[l]], axis=0),
            jnp.concatenate([z(1, FFN_CONV_W - 1, D_FF), cache_ffn_conv[l]], axis=0),
        )
        ys, sh_s, wk_s, cv_s, ff_s = _layer(xs, *st_s, wts, (nb * t_s, CHUNK, t_s, nb * t_s))
        bc = lambda a: jnp.broadcast_to(a[0:1], (bp,) + a.shape[1:])
        yp, sh_p, wk_p, cv_p, ff_p = _layer(xp, bc(sh_s), bc(wk_s), bc(cv_s), bc(ff_s), wts,
                                            _prompt_tiles(bp, t_prompt))
        outs_s.append((sh_s[1:], wk_s[1:], cv_s[1:], ff_s[1:]))
        outs_p.append((sh_p, wk_p, cv_p, ff_p))
        assert depth == 1
        xs, xp = ys, yp

    stack = lambda outs, i: jnp.stack([o[i] for o in outs]).astype(dt)
    y_prompt = xp
    y_sample = xs[1:]
    return (y_prompt, y_sample,
            stack(outs_p, 0), stack(outs_p, 1), stack(outs_p, 2), stack(outs_p, 3),
            stack(outs_s, 0), stack(outs_s, 1), stack(outs_s, 2), stack(outs_s, 3))
```
